```python
import jax, jax.numpy as jnp
from jax import lax
import numpy as np

D_MODEL = 1024
BATCH = 8
SEQ = 2048
DEPTH = 4
DEC_BATCH = 128
DEC_SEQ = 1
PAST_LEN = 2048
PAGE_SIZE = 128

N_MIXERS = 3
N_RET_LAYERS = (DEPTH + 2) // 3
N_POOL_LAYERS = (DEPTH + 1) // 3
N_ATT_LAYERS = DEPTH // 3

RET_HEADS = 4
RET_DK = D_MODEL // RET_HEADS
RET_DV = 2 * D_MODEL // RET_HEADS
RET_CHUNK = 128
RET_IN = 2 * RET_HEADS * RET_DK + 2 * RET_HEADS * RET_DV

POOL_WINDOWS = (2, 4, 8, 16)
POOL_GROUPS = len(POOL_WINDOWS)
POOL_GW = D_MODEL // POOL_GROUPS
POOL_PREV = max(POOL_WINDOWS) - 1

ATT_GROUPS = ((128, 1), (512, 4), (2048, 16))
ATT_HEADS = 16
ATT_DH = D_MODEL // ATT_HEADS
ATT_IN = len(ATT_GROUPS) * 3 * ATT_HEADS * ATT_DH

FFN_HIDDEN = ((8 * D_MODEL + 3 * 256 - 1) // (3 * 256)) * 256

NORM_EPS = 1e-6
GN_EPS = 1e-5
NEG_INF = -1e30

kernel_name = 'hybrid_retention_pool_dilated_decoder_step'


def rms_norm(x, g):
    xf = x.astype(jnp.float32)
    y = xf * lax.rsqrt(jnp.mean(xf * xf, axis=-1, keepdims=True) + NORM_EPS)
    return (y * g.astype(jnp.float32)).astype(x.dtype)


def swiglu(xn, w_in, w_out):
    gate, up = jnp.split(xn @ w_in, 2, axis=-1)
    return (jax.nn.silu(gate) * up) @ w_out


def ret_log_gamma():
    return jnp.log1p(-jnp.exp2(-5.0 - jnp.arange(RET_HEADS, dtype=jnp.float32)))


def alibi_slopes():
    n = len(ATT_GROUPS) * ATT_HEADS
    s = jnp.exp2(-8.0 * jnp.arange(1, n + 1, dtype=jnp.float32) / n)
    return s.reshape(len(ATT_GROUPS), ATT_HEADS)


def retention_scan(q, k, v, s0):
    B, L, H, _ = q.shape
    c = RET_CHUNK if L % RET_CHUNK == 0 else L
    nc = L // c
    lg = ret_log_gamma()
    n = jnp.arange(c, dtype=jnp.float32)
    diff = n[:, None] - n[None, :]
    decay = jnp.where(diff[None] >= 0,
                      jnp.exp(jnp.maximum(diff, 0.0)[None] * lg[:, None, None]), 0.0)
    xi = jnp.exp((n[:, None] + 1.0) * lg[None, :])
    zeta = jnp.exp((c - 1.0 - n)[:, None] * lg[None, :])
    g_c = jnp.exp(c * lg)

    def chunks(t):
        return t.astype(jnp.float32).reshape(B, nc, c, H, t.shape[-1]).transpose(1, 0, 2, 3, 4)

    def step(s, inp):
        qc, kc, vc = inp
        a = jnp.einsum('bnhk,bmhk->bhnm', qc, kc) * decay
        o = (jnp.einsum('bhnm,bmhv->bnhv', a, vc)
             + jnp.einsum('bnhk,bhkv->bnhv', qc, s) * xi[None, :, :, None])
        s = g_c[None, :, None, None] * s + jnp.einsum('bmhk,bmhv->bhkv', kc * zeta[None, :, :, None], vc)
        return s, o

    s, o = lax.scan(step, s0.astype(jnp.float32), (chunks(q), chunks(k), chunks(v)))
    o = o.transpose(1, 0, 2, 3, 4).reshape(B, L, H, RET_DV)
    return o, s


def retention_mixer(xn, s0, w_in, w_out):
    B, L, _ = xn.shape
    hk, hv = RET_HEADS * RET_DK, RET_HEADS * RET_DV
    q, k, v, g = jnp.split(xn @ w_in, [hk, 2 * hk, 2 * hk + hv], axis=-1)
    q = q.reshape(B, L, RET_HEADS, RET_DK) * (RET_DK ** -0.5)
    k = k.reshape(B, L, RET_HEADS, RET_DK)
    v = v.reshape(B, L, RET_HEADS, RET_DV)
    o, s = retention_scan(q, k, v, s0)
    mu = jnp.mean(o, axis=-1, keepdims=True)
    var = jnp.mean(jnp.square(o - mu), axis=-1, keepdims=True)
    o = (o - mu) * lax.rsqrt(var + GN_EPS)
    o = o.reshape(B, L, hv) * jax.nn.silu(g.astype(jnp.float32))
    return o.astype(xn.dtype) @ w_out, s


def pool_mixer(xn, prev, pos0, w_grp, scale):
    B, L, _ = xn.shape
    P = POOL_PREV
    xp = jnp.concatenate([prev.astype(jnp.float32), xn.astype(jnp.float32)], axis=1)
    cs = jnp.concatenate([jnp.zeros((B, 1, D_MODEL), jnp.float32), jnp.cumsum(xp, axis=1)], axis=1)
    pos = pos0 + jnp.arange(L)
    outs = []
    for gi, w in enumerate(POOL_WINDOWS):
        sl = slice(gi * POOL_GW, (gi + 1) * POOL_GW)
        wsum = cs[:, P + 1:P + 1 + L, sl] - cs[:, P + 1 - w:P + 1 - w + L, sl]
        cnt = jnp.minimum(pos + 1, w).astype(jnp.float32)
        outs.append(wsum / cnt[None, :, None] - xp[:, P:, sl])
    z = jnp.stack(outs, axis=2)
    y = jnp.einsum('blgc,gcd->blgd', z, w_grp.astype(jnp.float32)).reshape(B, L, D_MODEL)
    y = y * scale.astype(jnp.float32)
    return y.astype(xn.dtype), xp[:, -P:].astype(xn.dtype)


def dilated_band_prompt(q, k, v, dil, n_keys, slopes):
    B, S, H, Dh = q.shape
    Ls = S // dil
    b = n_keys
    nb = -(-Ls // b)
    Lp = nb * b

    def split(t):
        return t.reshape(B, Ls, dil, H, Dh).transpose(0, 2, 1, 3, 4)

    qb = jnp.pad(split(q), ((0, 0), (0, 0), (0, Lp - Ls), (0, 0), (0, 0))).reshape(B, dil, nb, b, H, Dh)

    def band(t):
        tp = jnp.pad(split(t), ((0, 0), (0, 0), (b, Lp - Ls), (0, 0), (0, 0)))
        prv = tp[:, :, :Lp].reshape(B, dil, nb, b, H, Dh)
        cur = tp[:, :, b:].reshape(B, dil, nb, b, H, Dh)
        return jnp.concatenate([prv, cur], axis=3)

    kb, vb = band(k), band(v)
    qi = jnp.arange(b)[:, None]
    j = jnp.arange(2 * b)[None, :]
    dist = qi + b - j
    blk = jnp.arange(nb)[:, None, None]
    mask = (dist >= 0) & (dist <= n_keys) & (blk * b - b + j >= 0)
    bias = -slopes[:, None, None] * (dil * dist).astype(jnp.float32)
    s = jnp.einsum('brnqhe,brnkhe->brnhqk', qb, kb) * (ATT_DH ** -0.5) + bias
    s = jnp.where(mask[:, None], s, NEG_INF)
    m = jnp.max(s, axis=-1)
    p = jnp.exp(s - m[..., None])
    l = jnp.sum(p, axis=-1)
    o = jnp.einsum('brnhqk,brnkhe->brnqhe', p, vb) / jnp.swapaxes(l, -1, -2)[..., None]

    def merge(t):
        tail = t.shape[4:]
        t = t.reshape((B, dil, Lp) + tail)[:, :, :Ls]
        return jnp.swapaxes(t, 1, 2).reshape((B, S) + tail)

    return merge(o), merge(jnp.swapaxes(m, -1, -2)), merge(jnp.swapaxes(l, -1, -2))


def dilated_gather_decode(q, kv_new, kv_cache, dil, n_keys, slopes):
    Wb = kv_cache.shape[1]
    T = q.shape[1]
    kk = jnp.arange(n_keys + 1)
    idx = Wb + jnp.arange(T)[:, None] - dil * kk[None, :]
    from_cache = kv_cache[:, jnp.clip(idx, 0, Wb - 1)]
    from_new = kv_new[:, jnp.clip(idx - Wb, 0, T - 1)]
    kvg = jnp.where((idx < Wb)[None, :, :, None, None, None], from_cache, from_new).astype(jnp.float32)
    s = (jnp.einsum('bthe,btkhe->bthk', q, kvg[:, :, :, 0]) * (ATT_DH ** -0.5)
         - slopes[:, None] * (dil * kk).astype(jnp.float32)[None, :])
    s = jnp.where((idx >= 0)[None, :, None, :], s, NEG_INF)
    m = jnp.max(s, axis=-1)
    p = jnp.exp(s - m[..., None])
    l = jnp.sum(p, axis=-1)
    o = jnp.einsum('bthk,btkhe->bthe', p, kvg[:, :, :, 1]) / l[..., None]
    return o, m, l


def attn_mixer(xn, kv_caches, w_in, w_out):
    B, L, _ = xn.shape
    G = len(ATT_GROUPS)
    proj = (xn @ w_in).reshape(B, L, G, 3, ATT_HEADS, ATT_DH)
    slopes = alibi_slopes()
    outs, ms, ls, new_kv = [], [], [], []
    for gi, (w, d) in enumerate(ATT_GROUPS):
        q = proj[:, :, gi, 0].astype(jnp.float32)
        kv = proj[:, :, gi, 1:]
        if kv_caches is None:
            o, m, l = dilated_band_prompt(q, kv[:, :, 0].astype(jnp.float32),
                                          kv[:, :, 1].astype(jnp.float32), d, w // d, slopes[gi])
            new_kv.append(kv[:, -min(w, L):])
        else:
            o, m, l = dilated_gather_decode(q, kv, kv_caches[gi], d, w // d, slopes[gi])
            new_kv.append(kv)
        outs.append(o)
        ms.append(m)
        ls.append(l)
    m_all = jnp.stack(ms)
    wts = jnp.stack(ls) * jnp.exp(m_all - jnp.max(m_all, axis=0))
    o = jnp.einsum('gblh,gblhe->blhe', wts, jnp.stack(outs)) / jnp.sum(wts, axis=0)[..., None]
    return o.reshape(B, L, D_MODEL).astype(xn.dtype) @ w_out, new_kv


def setup_inputs(seed: int = 0) -> dict:
    key = jax.random.key(seed)
    ks = jax.random.split(key, 16)
    d = D_MODEL

    def nrm(k, shape, scale=1.0):
        return jax.random.normal(k, shape, jnp.float32) * scale

    wb = [min(w, PAST_LEN) for (w, _) in ATT_GROUPS]
    return {
        'x_prompt': nrm(ks[0], (BATCH, SEQ, d)),
        'x_sample': nrm(ks[1], (DEC_BATCH, DEC_SEQ, d)),
        'state_ret': nrm(ks[2], (N_RET_LAYERS, DEC_BATCH, RET_HEADS, RET_DK, RET_DV)),
        'state_pool': nrm(ks[3], (N_POOL_LAYERS, DEC_BATCH, POOL_PREV, d)),
        'cache_kv_g0': nrm(ks[4], (N_ATT_LAYERS, DEC_BATCH, wb[0], 2, ATT_HEADS, ATT_DH)),
        'cache_kv_g1': nrm(ks[5], (N_ATT_LAYERS, DEC_BATCH, wb[1], 2, ATT_HEADS, ATT_DH)),
        'cache_kv_g2': nrm(ks[6], (N_ATT_LAYERS, DEC_BATCH, wb[2], 2, ATT_HEADS, ATT_DH)),
        'norm_gains': 1.0 + nrm(ks[7], (DEPTH, 4, d), 0.05),
        'w_in_ret': nrm(ks[8], (N_RET_LAYERS, d, RET_IN), d ** -0.5),
        'w_out_ret': nrm(ks[9], (N_RET_LAYERS, RET_HEADS * RET_DV, d), (RET_HEADS * RET_DV) ** -0.5),
        'w_grp_pool': nrm(ks[10], (N_POOL_LAYERS, POOL_GROUPS, POOL_GW, POOL_GW), POOL_GW ** -0.5),
        'scale_pool': 1.0 + nrm(ks[11], (N_POOL_LAYERS, d), 0.1),
        'w_in_attn': nrm(ks[12], (N_ATT_LAYERS, d, ATT_IN), d ** -0.5),
        'w_out_attn': nrm(ks[13], (N_ATT_LAYERS, d, d), d ** -0.5),
        'w_ffn_in': nrm(ks[14], (DEPTH, d, 2 * FFN_HIDDEN), d ** -0.5),
        'w_ffn_out': nrm(ks[15], (DEPTH, FFN_HIDDEN, d), FFN_HIDDEN ** -0.5),
    }


def reference(x_prompt, x_sample, state_ret, state_pool, cache_kv_g0, cache_kv_g1, cache_kv_g2,
              norm_gains, w_in_ret, w_out_ret, w_grp_pool, scale_pool,
              w_in_attn, w_out_attn, w_ffn_in, w_ffn_out):
    B = x_prompt.shape[0]
    caches = (cache_kv_g0, cache_kv_g1, cache_kv_g2)
    xp, xs = x_prompt, x_sample
    ret_p, ret_s, pool_p, pool_s = [], [], [], []
    kv_p = [[] for _ in ATT_GROUPS]
    kv_s = [[] for _ in ATT_GROUPS]
    for i in range(DEPTH):
        kind, j = i % N_MIXERS, i // N_MIXERS
        g = norm_gains[i]
        hp, hs = rms_norm(xp, g[0]), rms_norm(xs, g[0])
        if kind == 0:
            s0 = jnp.zeros((B, RET_HEADS, RET_DK, RET_DV), jnp.float32)
            mp, sp = retention_mixer(hp, s0, w_in_ret[j], w_out_ret[j])
            ms, ss = retention_mixer(hs, state_ret[j], w_in_ret[j], w_out_ret[j])
            ret_p.append(sp)
            ret_s.append(ss)
        elif kind == 1:
            prev0 = jnp.zeros((B, POOL_PREV, D_MODEL), hp.dtype)
            mp, sp = pool_mixer(hp, prev0, 0, w_grp_pool[j], scale_pool[j])
            ms, ss = pool_mixer(hs, state_pool[j], PAST_LEN, w_grp_pool[j], scale_pool[j])
            pool_p.append(sp)
            pool_s.append(ss)
        else:
            mp, kvp = attn_mixer(hp, None, w_in_attn[j], w_out_attn[j])
            ms, kvs = attn_mixer(hs, [c[j] for c in caches], w_in_attn[j], w_out_attn[j])
            for gi in range(len(ATT_GROUPS)):
                kv_p[gi].append(kvp[gi])
                kv_s[gi].append(kvs[gi])
        xp = xp + rms_norm(mp, g[1])
        xs = xs + rms_norm(ms, g[1])
        xp = xp + rms_norm(swiglu(rms_norm(xp, g[2]), w_ffn_in[i], w_ffn_out[i]), g[3])
        xs = xs + rms_norm(swiglu(rms_norm(xs, g[2]), w_ffn_in[i], w_ffn_out[i]), g[3])
    return (xp, xs,
            jnp.stack(ret_p), jnp.stack(ret_s),
            jnp.stack(pool_p), jnp.stack(pool_s),
            jnp.stack(kv_p[0]), jnp.stack(kv_s[0]),
            jnp.stack(kv_p[1]), jnp.stack(kv_s[1]),
            jnp.stack(kv_p[2]), jnp.stack(kv_s[2]))
```

```python
import functools

import jax
import jax.numpy as jnp
from jax import lax
from jax.experimental import pallas as pl
from jax.experimental.pallas import tpu as pltpu

F32 = jnp.float32
BF16 = jnp.bfloat16

D_MODEL = 1024
DEPTH = 4
PAST_LEN = 2048
N_MIXERS = 3

RET_HEADS = 4
RET_DK = D_MODEL // RET_HEADS
RET_DV = 2 * D_MODEL // RET_HEADS
RET_CHUNK = 128

POOL_WINDOWS = (2, 4, 8, 16)
POOL_GW = D_MODEL // len(POOL_WINDOWS)
POOL_PREV = max(POOL_WINDOWS) - 1
POOL_HALO = POOL_PREV + 1

ATT_GROUPS = ((128, 1), (512, 4), (2048, 16))
ATT_HEADS = 16
ATT_DH = D_MODEL // ATT_HEADS
ATT_GW = 3 * ATT_HEADS * ATT_DH
ATT_BLK = 128

FFN_HIDDEN = 2816

NORM_EPS = 1e-6
GN_EPS = 1e-5
NEG_INF = -1e30

LANES = 128
VMEM_LIMIT = 56 * 1024 * 1024


def _cparams(*sem):
    return pltpu.CompilerParams(dimension_semantics=sem, vmem_limit_bytes=VMEM_LIMIT)


def _rms(x, g):
    return x * lax.rsqrt(jnp.mean(x * x, axis=-1, keepdims=True) + NORM_EPS) * g


def _silu(x):
    return x * jax.nn.sigmoid(x)


def _norm_matmul_kernel(x_ref, g_ref, w_ref, o_ref, xn_ref):
    @pl.when(pl.program_id(1) == 0)
    def _():
        xn_ref[...] = _rms(x_ref[...], g_ref[...]).astype(BF16)

    o_ref[...] = jnp.dot(xn_ref[...], w_ref[...], preferred_element_type=F32).astype(o_ref.dtype)


def norm_matmul(x, g, w, out_dtype, tm, tn):
    m, d = x.shape
    n = w.shape[1]
    return pl.pallas_call(
        _norm_matmul_kernel,
        grid=(m // tm, n // tn),
        in_specs=[pl.BlockSpec((tm, d), lambda i, j: (i, 0)),
                  pl.BlockSpec((1, d), lambda i, j: (0, 0)),
                  pl.BlockSpec((d, tn), lambda i, j: (0, j))],
        out_specs=pl.BlockSpec((tm, tn), lambda i, j: (i, j)),
        out_shape=jax.ShapeDtypeStruct((m, n), out_dtype),
        scratch_shapes=[pltpu.VMEM((tm, d), BF16)],
        compiler_params=_cparams("parallel", "arbitrary"),
        name="norm_matmul",
    )(x, g, w)


def _matmul_norm_res_kernel(a_ref, w_ref, g_ref, x_ref, o_ref):
    y = jnp.dot(a_ref[...], w_ref[...], preferred_element_type=F32)
    o_ref[...] = x_ref[...] + _rms(y, g_ref[...])


def matmul_norm_res(a, w, g, x, tm):
    m, k = a.shape
    d = w.shape[1]
    return pl.pallas_call(
        _matmul_norm_res_kernel,
        grid=(m // tm,),
        in_specs=[pl.BlockSpec((tm, k), lambda i: (i, 0)),
                  pl.BlockSpec((k, d), lambda i: (0, 0)),
                  pl.BlockSpec((1, d), lambda i: (0, 0)),
                  pl.BlockSpec((tm, d), lambda i: (i, 0))],
        out_specs=pl.BlockSpec((tm, d), lambda i: (i, 0)),
        out_shape=jax.ShapeDtypeStruct((m, d), F32),
        compiler_params=_cparams("parallel"),
        name="matmul_norm_res",
    )(a, w, g, x)


def _ffn_kernel(x_ref, g2_ref, wg_ref, wu_ref, wo_ref, g3_ref, o_ref, xn_ref, acc_ref):
    j = pl.program_id(1)

    @pl.when(j == 0)
    def _():
        xn_ref[...] = _rms(x_ref[...], g2_ref[...]).astype(BF16)
        acc_ref[...] = jnp.zeros_like(acc_ref)

    xn = xn_ref[...]
    gate = jnp.dot(xn, wg_ref[...], preferred_element_type=F32)
    up = jnp.dot(xn, wu_ref[...], preferred_element_type=F32)
    a = (_silu(gate) * up).astype(BF16)
    acc_ref[...] += jnp.dot(a, wo_ref[...], preferred_element_type=F32)

    @pl.when(j == pl.num_programs(1) - 1)
    def _():
        o_ref[...] = x_ref[...] + _rms(acc_ref[...], g3_ref[...])


def ffn(x, g2, w_in, w_out, g3, tm, th):
    m, d = x.shape
    nh = FFN_HIDDEN // th
    return pl.pallas_call(
        _ffn_kernel,
        grid=(m // tm, nh),
        in_specs=[pl.BlockSpec((tm, d), lambda i, j: (i, 0)),
                  pl.BlockSpec((1, d), lambda i, j: (0, 0)),
                  pl.BlockSpec((d, th), lambda i, j: (0, j)),
                  pl.BlockSpec((d, th), lambda i, j: (0, j + nh)),
                  pl.BlockSpec((th, d), lambda i, j: (j, 0)),
                  pl.BlockSpec((1, d), lambda i, j: (0, 0))],
        out_specs=pl.BlockSpec((tm, d), lambda i, j: (i, 0)),
        out_shape=jax.ShapeDtypeStruct((m, d), F32),
        scratch_shapes=[pltpu.VMEM((tm, d), BF16), pltpu.VMEM((tm, d), F32)],
        compiler_params=_cparams("parallel", "arbitrary"),
        name="ffn",
    )(x, g2, w_in, w_in, w_out, g3)


def _ret_tables(c):
    lg = jnp.log1p(-jnp.exp2(-5.0 - jnp.arange(RET_HEADS, dtype=F32)))
    n = jnp.arange(c, dtype=F32)
    diff = n[:, None] - n[None, :]
    decay = jnp.where(diff[None] >= 0, jnp.exp(jnp.maximum(diff, 0.0)[None] * lg[:, None, None]), 0.0)
    xi = jnp.exp((n[:, None] + 1.0) * lg[None, :])
    zeta = jnp.exp((c - 1.0 - n)[:, None] * lg[None, :])
    g_c = jnp.exp(c * lg)
    return decay, xi.T[:, :, None], zeta.T[:, :, None], g_c[:, None, None]


def _group_norm_gate(o, gate):
    mu = jnp.mean(o, axis=-1, keepdims=True)
    oc = o - mu
    var = jnp.mean(oc * oc, axis=-1, keepdims=True)
    return oc * lax.rsqrt(var + GN_EPS) * _silu(gate)


def _ret_prompt_kernel(q_ref, k_ref, v_ref, gt_ref, decay_ref, xi_ref, zeta_ref, gc_ref, o_ref, s_ref):
    @pl.when(pl.program_id(2) == 0)
    def _():
        s_ref[...] = jnp.zeros_like(s_ref)

    q = q_ref[...] * (RET_DK ** -0.5)
    k = k_ref[...]
    v = v_ref[...]
    s = s_ref[...]
    a = lax.dot_general(q, k, (((1,), (1,)), ((), ())), preferred_element_type=F32) * decay_ref[...]
    o = (jnp.dot(a.astype(BF16), v, preferred_element_type=F32)
         + jnp.dot(q, s.astype(BF16), preferred_element_type=F32) * xi_ref[...])
    kz = (k.astype(F32) * zeta_ref[...]).T.astype(BF16)
    s_ref[...] = gc_ref[...] * s + jnp.dot(kz, v, preferred_element_type=F32)
    o_ref[...] = _group_norm_gate(o, gt_ref[...].astype(F32)).astype(o_ref.dtype)


def ret_prompt(proj, batch, seq):
    c = RET_CHUNK
    nc = seq // c
    hk = RET_HEADS * RET_DK
    kq, kv = hk // RET_DK, 2 * hk // RET_DV
    decay, xi, zeta, g_c = _ret_tables(c)
    tab = lambda shape: pl.BlockSpec((None,) + shape, lambda b, h, i: (h, 0, 0))
    return pl.pallas_call(
        _ret_prompt_kernel,
        grid=(batch, RET_HEADS, nc),
        in_specs=[pl.BlockSpec((c, RET_DK), lambda b, h, i: (b * nc + i, h)),
                  pl.BlockSpec((c, RET_DK), lambda b, h, i: (b * nc + i, kq + h)),
                  pl.BlockSpec((c, RET_DV), lambda b, h, i: (b * nc + i, kv + h)),
                  pl.BlockSpec((c, RET_DV), lambda b, h, i: (b * nc + i, kv + RET_HEADS + h)),
                  tab((c, c)), tab((c, 1)), tab((c, 1)), tab((1, 1))],
        out_specs=[pl.BlockSpec((c, RET_DV), lambda b, h, i: (b * nc + i, h)),
                   pl.BlockSpec((None, None, RET_DK, RET_DV), lambda b, h, i: (b, h, 0, 0))],
        out_shape=[jax.ShapeDtypeStruct((batch * seq, RET_HEADS * RET_DV), BF16),
                   jax.ShapeDtypeStruct((batch, RET_HEADS, RET_DK, RET_DV), F32)],
        compiler_params=_cparams("parallel", "parallel", "arbitrary"),
        name="ret_prompt",
    )(proj, proj, proj, proj, decay, xi, zeta, g_c)


def _ret_sample_kernel(bb, qk_ref, vg_ref, qkt_ref, s0_ref, xi_ref, gc_ref, o_ref, s_ref):
    hk = RET_HEADS * RET_DK
    hv = RET_HEADS * RET_DV
    scale = RET_DK ** -0.5
    for i in range(bb):
        for h in range(RET_HEADS):
            q_row = qk_ref[i:i + 1, h * RET_DK:(h + 1) * RET_DK] * scale
            k_row = qk_ref[i:i + 1, hk + h * RET_DK:hk + (h + 1) * RET_DK]
            v_row = vg_ref[i:i + 1, h * RET_DV:(h + 1) * RET_DV]
            g_row = vg_ref[i:i + 1, hv + h * RET_DV:hv + (h + 1) * RET_DV]
            q_col = qkt_ref[h * RET_DK:(h + 1) * RET_DK, i:i + 1] * scale
            k_col = qkt_ref[hk + h * RET_DK:hk + (h + 1) * RET_DK, i:i + 1]
            s0 = s0_ref[i, h]
            a = jnp.sum(q_row * k_row, axis=-1, keepdims=True)
            qs = jnp.sum(q_col * s0, axis=0, keepdims=True)
            o = a * v_row + qs * xi_ref[h]
            s_ref[i, h] = gc_ref[h] * s0 + k_col * v_row
            o_ref[i:i + 1, h * RET_DV:(h + 1) * RET_DV] = _group_norm_gate(o, g_row).astype(o_ref.dtype)


def ret_sample(proj, state, layer, bb=2):
    n = proj.shape[0]
    hk = RET_HEADS * RET_DK
    hv = RET_HEADS * RET_DV
    _, xi, _, g_c = _ret_tables(1)
    qk = proj[:, :2 * hk]
    vg = proj[:, 2 * hk:]
    qkt = qk.reshape(n // bb, bb, 2 * hk).transpose(0, 2, 1)
    qk3 = qk.reshape(n // bb, bb, 2 * hk)
    vg3 = vg.reshape(n // bb, bb, 2 * hv)
    o, s = pl.pallas_call(
        functools.partial(_ret_sample_kernel, bb),
        grid=(n // bb,),
        in_specs=[pl.BlockSpec((None, bb, 2 * hk), lambda i: (i, 0, 0)),
                  pl.BlockSpec((None, bb, 2 * hv), lambda i: (i, 0, 0)),
                  pl.BlockSpec((None, 2 * hk, bb), lambda i: (i, 0, 0)),
                  pl.BlockSpec((None, bb, RET_HEADS, RET_DK, RET_DV), lambda i: (layer, i, 0, 0, 0)),
                  pl.BlockSpec((RET_HEADS, 1, 1), lambda i: (0, 0, 0)),
                  pl.BlockSpec((RET_HEADS, 1, 1), lambda i: (0, 0, 0))],
        out_specs=[pl.BlockSpec((None, bb, hv), lambda i: (i, 0, 0)),
                   pl.BlockSpec((bb, RET_HEADS, RET_DK, RET_DV), lambda i: (i, 0, 0, 0))],
        out_shape=[jax.ShapeDtypeStruct((n // bb, bb, hv), BF16),
                   jax.ShapeDtypeStruct(state.shape[1:], F32)],
        compiler_params=_cparams("parallel"),
        name="ret_sample",
    )(qk3, vg3, qkt, state, xi, g_c)
    return o.reshape(n, hv), s


def _pool_prompt_kernel(tl, x_ref, halo_ref, g0_ref, w_ref, sc_ref, g1_ref, o_ref, last_ref, buf_ref):
    i = pl.program_id(1)
    x = x_ref[...]
    xn = _rms(x, g0_ref[...])
    hn = _rms(halo_ref[...], g0_ref[...])
    buf_ref[0:POOL_HALO, :] = jnp.where(i > 0, hn, 0.0)
    buf_ref[POOL_HALO:POOL_HALO + tl, :] = xn
    pos = i * tl + lax.broadcasted_iota(jnp.int32, (tl, 1), 0)
    ys = []
    for gi, w in enumerate(POOL_WINDOWS):
        cols = slice(gi * POOL_GW, (gi + 1) * POOL_GW)
        cur = xn[:, cols]
        acc = cur
        for j in range(1, w):
            acc = acc + buf_ref[POOL_HALO - j:POOL_HALO - j + tl, cols]
        cnt = jnp.minimum(pos + 1, w).astype(F32)
        z = acc / cnt - cur
        ys.append(jnp.dot(z.astype(BF16), w_ref[gi], preferred_element_type=F32))
    y = jnp.concatenate(ys, axis=1) * sc_ref[...]
    o_ref[...] = x + _rms(y, g1_ref[...])
    last_ref[...] = buf_ref[tl:tl + POOL_HALO, :]


def pool_prompt(x, g0, w_grp, scale, g1, tl=512):
    batch, seq, d = x.shape
    hb = tl // POOL_HALO
    vec = pl.BlockSpec((1, d), lambda b, i: (0, 0))
    return pl.pallas_call(
        functools.partial(_pool_prompt_kernel, tl),
        grid=(batch, seq // tl),
        in_specs=[pl.BlockSpec((None, tl, d), lambda b, i: (b, i, 0)),
                  pl.BlockSpec((None, POOL_HALO, d), lambda b, i: (b, jnp.maximum(i * hb - 1, 0), 0)),
                  vec,
                  pl.BlockSpec(w_grp.shape, lambda b, i: (0, 0, 0)),
                  vec, vec],
        out_specs=[pl.BlockSpec((None, tl, d), lambda b, i: (b, i, 0)),
                   pl.BlockSpec((None, POOL_HALO, d), lambda b, i: (b, 0, 0))],
        out_shape=[jax.ShapeDtypeStruct(x.shape, F32),
                   jax.ShapeDtypeStruct((batch, POOL_HALO, d), F32)],
        scratch_shapes=[pltpu.VMEM((POOL_HALO + tl, d), F32)],
        compiler_params=_cparams("parallel", "arbitrary"),
        name="pool_prompt",
    )(x, x, g0, w_grp, scale, g1)


def _pool_sample_kernel(x_ref, st_ref, g0_ref, w_ref, sc_ref, g1_ref, o_ref, nst_ref):
    d = D_MODEL
    x = x_ref[...]
    xn = _rms(x, g0_ref[...])
    ys = []
    for gi, w in enumerate(POOL_WINDOWS):
        acc = xn[:, gi * POOL_GW:(gi + 1) * POOL_GW]
        cur = acc
        for j in range(1, w):
            base = (POOL_PREV - j) * d + gi * POOL_GW
            acc = acc + st_ref[:, base:base + POOL_GW]
        cnt = float(min(PAST_LEN + 1, w))
        z = acc / cnt - cur
        ys.append(jnp.dot(z.astype(BF16), w_ref[gi], preferred_element_type=F32))
    y = jnp.concatenate(ys, axis=1) * sc_ref[...]
    o_ref[...] = x + _rms(y, g1_ref[...])
    nst_ref[:, :(POOL_PREV - 1) * d] = st_ref[:, d:]
    nst_ref[:, (POOL_PREV - 1) * d:] = xn


def pool_sample(x, state, g0, w_grp, scale, g1, tb=32):
    n, d = x.shape
    vec = pl.BlockSpec((1, d), lambda i: (0, 0))
    return pl.pallas_call(
        _pool_sample_kernel,
        grid=(n // tb,),
        in_specs=[pl.BlockSpec((tb, d), lambda i: (i, 0)),
                  pl.BlockSpec((tb, POOL_PREV * d), lambda i: (i, 0)),
                  vec,
                  pl.BlockSpec(w_grp.shape, lambda i: (0, 0, 0)),
                  vec, vec],
        out_specs=[pl.BlockSpec((tb, d), lambda i: (i, 0)),
                   pl.BlockSpec((tb, POOL_PREV * d), lambda i: (i, 0))],
        out_shape=[jax.ShapeDtypeStruct((n, d), F32),
                   jax.ShapeDtypeStruct(state.shape, F32)],
        compiler_params=_cparams("parallel"),
        name="pool_sample",
    )(x, state, g0, w_grp, scale, g1)


def _attn_proj_kernel(dil, kv_rows, nq, x_ref, g_ref, w_ref, perm_ref, kv_ref, xn_ref, res_ref):
    j = pl.program_id(1)
    seq, tn = perm_ref.shape
    ls = seq // dil

    @pl.when(j == 0)
    def _():
        xn_ref[...] = _rms(x_ref[...], g_ref[...]).astype(BF16)

    res = jnp.dot(xn_ref[...], w_ref[...], preferred_element_type=F32)
    if dil == 1:
        perm_ref[...] = res.astype(BF16)
    else:
        for c in range(tn // LANES):
            res_ref[c] = res[:, c * LANES:(c + 1) * LANES]
        for c in range(tn // LANES):
            for r in range(dil):
                perm_ref[r * ls:(r + 1) * ls, c * LANES:(c + 1) * LANES] = (
                    res_ref[c, pl.ds(r, ls, stride=dil), :].astype(BF16))

    @pl.when(j >= nq)
    def _():
        kv_ref[...] = res[seq - kv_rows:, :]


def attn_proj(x, g, w, gi, tn=256):
    batch, seq, d = x.shape
    win, dil = ATT_GROUPS[gi]
    kv_rows = min(win, seq)
    nq = ATT_HEADS * ATT_DH // tn
    ncol = ATT_GW // tn
    return pl.pallas_call(
        functools.partial(_attn_proj_kernel, dil, kv_rows, nq),
        grid=(batch, ncol),
        in_specs=[pl.BlockSpec((None, seq, d), lambda b, j: (b, 0, 0)),
                  pl.BlockSpec((1, d), lambda b, j: (0, 0)),
                  pl.BlockSpec((d, tn), lambda b, j: (0, gi * ncol + j))],
        out_specs=[pl.BlockSpec((None, seq, tn), lambda b, j: (b, 0, j)),
                   pl.BlockSpec((None, kv_rows, tn), lambda b, j: (b, 0, jnp.maximum(j - nq, 0)))],
        out_shape=[jax.ShapeDtypeStruct((batch, seq, ATT_GW), BF16),
                   jax.ShapeDtypeStruct((batch, kv_rows, ATT_GW - nq * tn), F32)],
        scratch_shapes=[pltpu.VMEM((seq, d), BF16), pltpu.VMEM((tn // LANES, seq, LANES), F32)],
        compiler_params=_cparams("parallel", "arbitrary"),
        name="attn_proj_g%d" % gi,
    )(x, g, w)


def _alibi_slope(group, head):
    n = len(ATT_GROUPS) * ATT_HEADS
    idx = jnp.full((ATT_BLK, ATT_BLK), group * ATT_HEADS + 1, F32) + head
    return jnp.exp2(-8.0 * idx / n)


def _attn_prompt_kernel(seq, *refs):
    ng = len(ATT_GROUPS)
    qkv = [refs[3 * g:3 * g + 3] for g in range(ng)]
    out_ref, o_s, m_s, l_s = refs[3 * ng:]
    hp = pl.program_id(1)
    lane = lax.broadcasted_iota(jnp.int32, (ATT_BLK, LANES), 1)
    first_head = lane < ATT_DH
    qi = lax.broadcasted_iota(jnp.int32, (ATT_BLK, ATT_BLK), 0)
    kj = lax.broadcasted_iota(jnp.int32, (ATT_BLK, ATT_BLK), 1)
    scale = ATT_DH ** -0.5
    nt = (((1,), (1,)), ((), ()))

    for g, (win, dil) in enumerate(ATT_GROUPS):
        q_ref, k_ref, v_ref = qkv[g]
        n_keys = win // dil
        ls = seq // dil
        nb = ls // ATT_BLK
        bias_cur, bias_prev = [], []
        for hh in range(2):
            slope = _alibi_slope(g, (hp * 2 + hh).astype(F32))
            d_cur = qi - kj
            d_prev = qi + ATT_BLK - kj
            bias_cur.append(jnp.where(d_cur >= 0, -slope * (dil * d_cur).astype(F32), NEG_INF))
            bias_prev.append(jnp.where(d_prev <= n_keys, -slope * (dil * d_prev).astype(F32), NEG_INF))

        def block(qb, carry, g=g, dil=dil, nb=nb, q_ref=q_ref, k_ref=k_ref, v_ref=v_ref,
                  bias_cur=bias_cur, bias_prev=bias_prev):
            blk = qb % nb
            res = qb // nb
            row = pl.multiple_of(qb * ATT_BLK, ATT_BLK)
            prow = pl.multiple_of(jnp.maximum(qb - 1, 0) * ATT_BLK, ATT_BLK)
            qf = q_ref[pl.ds(row, ATT_BLK), :].astype(F32)
            k_cur = k_ref[pl.ds(row, ATT_BLK), :]
            v_cur = v_ref[pl.ds(row, ATT_BLK), :]
            has_prev = nb > 1
            if has_prev:
                k_prev = k_ref[pl.ds(prow, ATT_BLK), :]
                v_prev = v_ref[pl.ds(prow, ATT_BLK), :]
                prev_ok = blk > 0
            os, ms, lsums = [], [], []
            for hh in range(2):
                head_mask = first_head if hh == 0 else jnp.logical_not(first_head)
                qm = jnp.where(head_mask, qf, 0.0).astype(BF16)
                s_cur = lax.dot_general(qm, k_cur, nt, preferred_element_type=F32) * scale + bias_cur[hh]
                m = jnp.max(s_cur, axis=-1, keepdims=True)
                if has_prev:
                    s_prev = lax.dot_general(qm, k_prev, nt, preferred_element_type=F32) * scale + bias_prev[hh]
                    s_prev = jnp.where(prev_ok, s_prev, NEG_INF)
                    m = jnp.maximum(m, jnp.max(s_prev, axis=-1, keepdims=True))
                p_cur = jnp.exp(s_cur - m)
                l = jnp.sum(p_cur, axis=-1, keepdims=True)
                o = jnp.dot(p_cur.astype(BF16), v_cur, preferred_element_type=F32)
                if has_prev:
                    p_prev = jnp.exp(s_prev - m)
                    l = l + jnp.sum(p_prev, axis=-1, keepdims=True)
                    o = o + jnp.dot(p_prev.astype(BF16), v_prev, preferred_element_type=F32)
                os.append(o)
                ms.append(m)
                lsums.append(l)
            start = blk * (ATT_BLK * dil) + res
            dst = pl.ds(start, ATT_BLK, stride=dil) if dil > 1 else pl.ds(row, ATT_BLK)
            o_s[g, dst, :] = jnp.where(first_head, os[0], os[1])
            m_s[g, dst, :] = jnp.where(first_head, ms[0], ms[1])
            l_s[g, dst, :] = jnp.where(first_head, lsums[0], lsums[1])
            return carry

        lax.fori_loop(0, seq // ATT_BLK, block, 0)

    m_all = m_s[0]
    for g in range(1, ng):
        m_all = jnp.maximum(m_all, m_s[g])
    num = jnp.zeros((seq, LANES), F32)
    den = jnp.zeros((seq, LANES), F32)
    for g in range(ng):
        e = jnp.exp(m_s[g] - m_all)
        num = num + e * o_s[g]
        den = den + e * l_s[g]
    out_ref[...] = (num / den).astype(out_ref.dtype)


def attn_prompt(qkv_groups, batch, seq):
    ng = len(ATT_GROUPS)
    hpairs = ATT_HEADS * ATT_DH // LANES
    in_specs, args = [], []
    for g in range(ng):
        for part in range(3):
            in_specs.append(pl.BlockSpec((seq, LANES), lambda b, h, part=part: (b, part * hpairs + h)))
            args.append(qkv_groups[g])
    return pl.pallas_call(
        functools.partial(_attn_prompt_kernel, seq),
        grid=(batch, hpairs),
        in_specs=in_specs,
        out_specs=pl.BlockSpec((seq, LANES), lambda b, h: (b, h)),
        out_shape=jax.ShapeDtypeStruct((batch * seq, D_MODEL), BF16),
        scratch_shapes=[pltpu.VMEM((ng, seq, LANES), F32)] * 3,
        compiler_params=_cparams("parallel", "parallel"),
        name="attn_prompt",
    )(*args)


def _attn_sample_kernel(n_keys, q_ref, kvn_ref, slope_ref, c0_ref, c1_ref, c2_ref, o_ref, m_s, l_s, acc_s):
    g = pl.program_id(0)
    t = pl.program_id(1)
    caches = (c0_ref, c1_ref, c2_ref)
    q = q_ref[g]
    scale = ATT_DH ** -0.5
    dil = jnp.where(g == 0, ATT_GROUPS[0][1], jnp.where(g == 1, ATT_GROUPS[1][1], ATT_GROUPS[2][1]))
    bias = -slope_ref[g] * (dil * t).astype(F32)

    @pl.when(jnp.logical_and(g == 0, t == 0))
    def _():
        m_s[...] = jnp.full_like(m_s, NEG_INF)
        l_s[...] = jnp.zeros_like(l_s)
        acc_s[...] = jnp.zeros_like(acc_s)

    def update(k, v):
        s = jnp.sum(q * k, axis=-1, keepdims=True) * scale + bias
        m_old = m_s[...]
        m_new = jnp.maximum(m_old, s)
        a = jnp.exp(m_old - m_new)
        p = jnp.exp(s - m_new)
        m_s[...] = m_new
        l_s[...] = a * l_s[...] + p
        acc_s[...] = a * acc_s[...] + p * v

    @pl.when(t == 0)
    def _():
        update(kvn_ref[g, :, 0], kvn_ref[g, :, 1])

    for gi in range(len(ATT_GROUPS)):
        @pl.when(jnp.logical_and(t > 0, g == gi))
        def _(gi=gi):
            update(caches[gi][:, 0], caches[gi][:, 1])

    @pl.when(jnp.logical_and(g == pl.num_programs(0) - 1, t == n_keys))
    def _():
        o_ref[...] = (acc_s[...] / l_s[...]).astype(o_ref.dtype)


def attn_sample(q, kv_new, caches, layer):
    ng, n = q.shape[0], q.shape[1]
    n_keys = ATT_GROUPS[0][0] // ATT_GROUPS[0][1]
    nall = ng * ATT_HEADS
    slopes = jnp.exp2(-8.0 * jnp.arange(1, nall + 1, dtype=F32) / nall).reshape(ng, ATT_HEADS, 1)

    def cache_spec(gi):
        win, dil = ATT_GROUPS[gi]
        assert caches[gi].shape[2] == dil * n_keys and win // dil == n_keys

        def index(g, t):
            row = dil * (n_keys - jnp.clip(t, 1, n_keys))
            row = jnp.where(g < gi, dil * (n_keys - 1), jnp.where(g > gi, 0, row))
            return (layer, 0, row, 0, 0, 0)
        return pl.BlockSpec((None, n, None, 2, ATT_HEADS, ATT_DH), index)

    full = lambda a: pl.BlockSpec(a.shape, lambda g, t: (0,) * a.ndim)
    return pl.pallas_call(
        functools.partial(_attn_sample_kernel, n_keys),
        grid=(ng, n_keys + 1),
        in_specs=[full(q), full(kv_new), full(slopes)] + [cache_spec(gi) for gi in range(ng)],
        out_specs=pl.BlockSpec((n, ATT_HEADS, ATT_DH), lambda g, t: (0, 0, 0)),
        out_shape=jax.ShapeDtypeStruct((n, ATT_HEADS, ATT_DH), BF16),
        scratch_shapes=[pltpu.VMEM((n, ATT_HEADS, 1), F32), pltpu.VMEM((n, ATT_HEADS, 1), F32),
                        pltpu.VMEM((n, ATT_HEADS, ATT_DH), F32)],
        compiler_params=_cparams("arbitrary", "arbitrary"),
        name="attn_sample",
    )(q, kv_new, slopes, *caches)


def kernel(x_prompt, x_sample, state_ret, state_pool, cache_kv_g0, cache_kv_g1, cache_kv_g2, norm_gains,
           w_in_ret, w_out_ret, w_grp_pool, scale_pool, w_in_attn, w_out_attn, w_ffn_in, w_ffn_out):
    batch, seq, d = x_prompt.shape
    ns = x_sample.shape[0]
    caches = (cache_kv_g0, cache_kv_g1, cache_kv_g2)
    ng = len(ATT_GROUPS)
    xp = x_prompt.reshape(batch * seq, d)
    xs = x_sample.reshape(ns, d)
    ret_p, ret_s, pool_p, pool_s = [], [], [], []
    kv_p = [[] for _ in range(ng)]
    kv_s = [[] for _ in range(ng)]

    for i in range(DEPTH):
        kind, j = i % N_MIXERS, i // N_MIXERS
        g = [norm_gains[i, r][None, :] for r in range(4)]
        if kind == 0:
            w_in = w_in_ret[j].astype(BF16)
            w_out = w_out_ret[j].astype(BF16)
            proj = norm_matmul(xp, g[0], w_in, BF16, 1024, 1024)
            o, sp = ret_prompt(proj, batch, seq)
            xp = matmul_norm_res(o, w_out, g[1], xp, 512)
            proj_s = norm_matmul(xs, g[0], w_in, F32, ns, 1024)
            o_s, ss = ret_sample(proj_s, state_ret, j)
            xs = matmul_norm_res(o_s, w_out, g[1], xs, ns)
            ret_p.append(sp)
            ret_s.append(ss)
        elif kind == 1:
            w_grp = w_grp_pool[j].astype(BF16)
            scale = scale_pool[j][None, :]
            xp3, last = pool_prompt(xp.reshape(batch, seq, d), g[0], w_grp, scale, g[1])
            xp = xp3.reshape(batch * seq, d)
            xs, nst = pool_sample(xs, state_pool[j].reshape(ns, POOL_PREV * d), g[0], w_grp, scale, g[1])
            pool_p.append(last[:, POOL_HALO - POOL_PREV:])
            pool_s.append(nst.reshape(ns, POOL_PREV, d))
        else:
            w_in = w_in_attn[j].astype(BF16)
            w_out = w_out_attn[j].astype(BF16)
            x3 = xp.reshape(batch, seq, d)
            qkv = []
            for gi in range(ng):
                perm, kv = attn_proj(x3, g[0], w_in, gi)
                qkv.append(perm.reshape(batch * seq, ATT_GW))
                kv_p[gi].append(kv.reshape(batch, kv.shape[1], 2, ATT_HEADS, ATT_DH))
            a = attn_prompt(qkv, batch, seq)
            xp = matmul_norm_res(a, w_out, g[1], xp, 512)
            proj_s = norm_matmul(xs, g[0], w_in, F32, ns, 1024)
            p5 = proj_s.reshape(ns, ng, 3, ATT_HEADS, ATT_DH)
            q_s = p5[:, :, 0].transpose(1, 0, 2, 3)
            kvn = p5[:, :, 1:].transpose(1, 0, 2, 3, 4)
            a_s = attn_sample(q_s, kvn, caches, j)
            xs = matmul_norm_res(a_s.reshape(ns, d), w_out, g[1], xs, ns)
            for gi in range(ng):
                kv_s[gi].append(kvn[gi][:, None])
        w_fi = w_ffn_in[i].astype(BF16)
        w_fo = w_ffn_out[i].astype(BF16)
        xp = ffn(xp, g[2], w_fi, w_fo, g[3], 1024, 256)
        xs = ffn(xs, g[2], w_fi, w_fo, g[3], ns, 256)

    return (xp.reshape(batch, seq, d), xs.reshape(ns, 1, d),
            jnp.stack(ret_p), jnp.stack(ret_s),
            jnp.stack(pool_p), jnp.stack(pool_s),
            jnp.stack(kv_p[0]), jnp.stack(kv_s[0]),
            jnp.stack(kv_p[1]), jnp.stack(kv_s[1]),
            jnp.stack(kv_p[2]), jnp.stack(kv_s[2]))
```

```python
import functools

import jax
import jax.numpy as jnp
from jax import lax
from jax.experimental import pallas as pl
from jax.experimental.pallas import tpu as pltpu

F32 = jnp.float32
BF16 = jnp.bfloat16

D_MODEL = 1024
DEPTH = 4
PAST_LEN = 2048
N_MIXERS = 3

RET_HEADS = 4
RET_DK = D_MODEL // RET_HEADS
RET_DV = 2 * D_MODEL // RET_HEADS
RET_CHUNK = 128

POOL_WINDOWS = (2, 4, 8, 16)
POOL_GW = D_MODEL // len(POOL_WINDOWS)
POOL_PREV = max(POOL_WINDOWS) - 1
POOL_HALO = POOL_PREV + 1

ATT_GROUPS = ((128, 1), (512, 4), (2048, 16))
ATT_HEADS = 16
ATT_DH = D_MODEL // ATT_HEADS
ATT_GW = 3 * ATT_HEADS * ATT_DH
ATT_BLK = 128
ATT_INTERLEAVE = 4

FFN_HIDDEN = 2816

NORM_EPS = 1e-6
GN_EPS = 1e-5
NEG_INF = -1e30

LANES = 128
VMEM_LIMIT = 56 * 1024 * 1024


def _cparams(*sem):
    return pltpu.CompilerParams(dimension_semantics=sem, vmem_limit_bytes=VMEM_LIMIT)


def _rms(x, g):
    return x * lax.rsqrt(jnp.mean(x * x, axis=-1, keepdims=True) + NORM_EPS) * g


def _silu(x):
    return x * jax.nn.sigmoid(x)


def _norm_matmul_kernel(x_ref, g_ref, w_ref, o_ref, xn_ref):
    @pl.when(pl.program_id(1) == 0)
    def _():
        xn_ref[...] = _rms(x_ref[...], g_ref[...]).astype(BF16)

    o_ref[...] = jnp.dot(xn_ref[...], w_ref[...], preferred_element_type=F32).astype(o_ref.dtype)


def norm_matmul(x, g, w, out_dtype, tm, tn):
    m, d = x.shape
    n = w.shape[1]
    return pl.pallas_call(
        _norm_matmul_kernel,
        grid=(m // tm, n // tn),
        in_specs=[pl.BlockSpec((tm, d), lambda i, j: (i, 0)),
                  pl.BlockSpec((1, d), lambda i, j: (0, 0)),
                  pl.BlockSpec((d, tn), lambda i, j: (0, j))],
        out_specs=pl.BlockSpec((tm, tn), lambda i, j: (i, j)),
        out_shape=jax.ShapeDtypeStruct((m, n), out_dtype),
        scratch_shapes=[pltpu.VMEM((tm, d), BF16)],
        compiler_params=_cparams("parallel", "arbitrary"),
        name="norm_matmul",
    )(x, g, w)


def _matmul_norm_res_kernel(a_ref, w_ref, g_ref, x_ref, o_ref):
    y = jnp.dot(a_ref[...], w_ref[...], preferred_element_type=F32)
    o_ref[...] = x_ref[...] + _rms(y, g_ref[...])


def matmul_norm_res(a, w, g, x, tm):
    m, k = a.shape
    d = w.shape[1]
    return pl.pallas_call(
        _matmul_norm_res_kernel,
        grid=(m // tm,),
        in_specs=[pl.BlockSpec((tm, k), lambda i: (i, 0)),
                  pl.BlockSpec((k, d), lambda i: (0, 0)),
                  pl.BlockSpec((1, d), lambda i: (0, 0)),
                  pl.BlockSpec((tm, d), lambda i: (i, 0))],
        out_specs=pl.BlockSpec((tm, d), lambda i: (i, 0)),
        out_shape=jax.ShapeDtypeStruct((m, d), F32),
        compiler_params=_cparams("parallel"),
        name="matmul_norm_res",
    )(a, w, g, x)


def _ffn_kernel(x_ref, g2_ref, wg_ref, wu_ref, wo_ref, g3_ref, o_ref, xn_ref, acc_ref):
    j = pl.program_id(1)

    @pl.when(j == 0)
    def _():
        xn_ref[...] = _rms(x_ref[...], g2_ref[...]).astype(BF16)
        acc_ref[...] = jnp.zeros_like(acc_ref)

    xn = xn_ref[...]
    gate = jnp.dot(xn, wg_ref[...], preferred_element_type=F32)
    up = jnp.dot(xn, wu_ref[...], preferred_element_type=F32)
    a = (_silu(gate) * up).astype(BF16)
    acc_ref[...] += jnp.dot(a, wo_ref[...], preferred_element_type=F32)

    @pl.when(j == pl.num_programs(1) - 1)
    def _():
        o_ref[...] = x_ref[...] + _rms(acc_ref[...], g3_ref[...])


def ffn(x, g2, w_in, w_out, g3, tm, th):
    m, d = x.shape
    nh = FFN_HIDDEN // th
    return pl.pallas_call(
        _ffn_kernel,
        grid=(m // tm, nh),
        in_specs=[pl.BlockSpec((tm, d), lambda i, j: (i, 0)),
                  pl.BlockSpec((1, d), lambda i, j: (0, 0)),
                  pl.BlockSpec((d, th), lambda i, j: (0, j)),
                  pl.BlockSpec((d, th), lambda i, j: (0, j + nh)),
                  pl.BlockSpec((th, d), lambda i, j: (j, 0)),
                  pl.BlockSpec((1, d), lambda i, j: (0, 0))],
        out_specs=pl.BlockSpec((tm, d), lambda i, j: (i, 0)),
        out_shape=jax.ShapeDtypeStruct((m, d), F32),
        scratch_shapes=[pltpu.VMEM((tm, d), BF16), pltpu.VMEM((tm, d), F32)],
        compiler_params=_cparams("parallel", "arbitrary"),
        name="ffn",
    )(x, g2, w_in, w_in, w_out, g3)


def _ret_tables(c):
    lg = jnp.log1p(-jnp.exp2(-5.0 - jnp.arange(RET_HEADS, dtype=F32)))
    n = jnp.arange(c, dtype=F32)
    diff = n[:, None] - n[None, :]
    decay = jnp.where(diff[None] >= 0, jnp.exp(jnp.maximum(diff, 0.0)[None] * lg[:, None, None]), 0.0)
    xi = jnp.exp((n[:, None] + 1.0) * lg[None, :])
    zeta = jnp.exp((c - 1.0 - n)[:, None] * lg[None, :])
    g_c = jnp.exp(c * lg)
    return decay, xi.T[:, :, None], zeta.T[:, :, None], g_c[:, None, None]


def _group_norm_gate(o, gate):
    mu = jnp.mean(o, axis=-1, keepdims=True)
    oc = o - mu
    var = jnp.mean(oc * oc, axis=-1, keepdims=True)
    return oc * lax.rsqrt(var + GN_EPS) * _silu(gate)


def _ret_prompt_kernel(nsub, p_ref, decay_ref, xi_ref, zeta_ref, gc_ref, o_ref, s_ref):
    @pl.when(pl.program_id(1) == 0)
    def _():
        s_ref[...] = jnp.zeros_like(s_ref)

    c = RET_CHUNK
    hk = RET_HEADS * RET_DK
    hv = RET_HEADS * RET_DV
    nt = (((1,), (1,)), ((), ()))
    for h in range(RET_HEADS):
        s = s_ref[h]
        for cc in range(nsub):
            rows = slice(cc * c, (cc + 1) * c)
            q = p_ref[rows, h * RET_DK:(h + 1) * RET_DK] * (RET_DK ** -0.5)
            k = p_ref[rows, hk + h * RET_DK:hk + (h + 1) * RET_DK]
            v = p_ref[rows, 2 * hk + h * RET_DV:2 * hk + (h + 1) * RET_DV]
            gt = p_ref[rows, 2 * hk + hv + h * RET_DV:2 * hk + hv + (h + 1) * RET_DV]
            a = lax.dot_general(q, k, nt, preferred_element_type=F32) * decay_ref[h]
            o = (jnp.dot(a.astype(BF16), v, preferred_element_type=F32)
                 + jnp.dot(q, s.astype(BF16), preferred_element_type=F32) * xi_ref[h])
            kz = (k.astype(F32) * zeta_ref[h]).T.astype(BF16)
            s = gc_ref[h] * s + jnp.dot(kz, v, preferred_element_type=F32)
            o_ref[rows, h * RET_DV:(h + 1) * RET_DV] = _group_norm_gate(o, gt.astype(F32)).astype(o_ref.dtype)
        s_ref[h] = s


def ret_prompt(proj, batch, seq, nsub=2):
    rows = nsub * RET_CHUNK
    nstep = seq // rows
    decay, xi, zeta, g_c = _ret_tables(RET_CHUNK)
    full = lambda a: pl.BlockSpec(a.shape, lambda b, i: (0,) * a.ndim)
    return pl.pallas_call(
        functools.partial(_ret_prompt_kernel, nsub),
        grid=(batch, nstep),
        in_specs=[pl.BlockSpec((rows, proj.shape[1]), lambda b, i: (b * nstep + i, 0)),
                  full(decay), full(xi), full(zeta), full(g_c)],
        out_specs=[pl.BlockSpec((rows, RET_HEADS * RET_DV), lambda b, i: (b * nstep + i, 0)),
                   pl.BlockSpec((None, RET_HEADS, RET_DK, RET_DV), lambda b, i: (b, 0, 0, 0))],
        out_shape=[jax.ShapeDtypeStruct((batch * seq, RET_HEADS * RET_DV), BF16),
                   jax.ShapeDtypeStruct((batch, RET_HEADS, RET_DK, RET_DV), F32)],
        compiler_params=_cparams("parallel", "arbitrary"),
        name="ret_prompt",
    )(proj, decay, xi, zeta, g_c)


def _ret_sample_kernel(bb, qk_ref, vg_ref, qkt_ref, s0_ref, xi_ref, gc_ref, *rest):
    o_ref, s_ref = rest[-2:]
    hk = RET_HEADS * RET_DK
    hv = RET_HEADS * RET_DV
    scale = RET_DK ** -0.5
    for i in range(bb):
        for h in range(RET_HEADS):
            q_row = qk_ref[i:i + 1, h * RET_DK:(h + 1) * RET_DK] * scale
            k_row = qk_ref[i:i + 1, hk + h * RET_DK:hk + (h + 1) * RET_DK]
            v_row = vg_ref[i:i + 1, h * RET_DV:(h + 1) * RET_DV]
            g_row = vg_ref[i:i + 1, hv + h * RET_DV:hv + (h + 1) * RET_DV]
            q_col = qkt_ref[h * RET_DK:(h + 1) * RET_DK, i:i + 1] * scale
            k_col = qkt_ref[hk + h * RET_DK:hk + (h + 1) * RET_DK, i:i + 1]
            s0 = s0_ref[i, h]
            a = jnp.sum(q_row * k_row, axis=-1, keepdims=True)
            qs = jnp.sum(q_col * s0, axis=0, keepdims=True)
            o = a * v_row + qs * xi_ref[h]
            s_ref[i, h] = gc_ref[h] * s0 + k_col * v_row
            o_ref[i:i + 1, h * RET_DV:(h + 1) * RET_DV] = _group_norm_gate(o, g_row).astype(o_ref.dtype)


def ret_sample(proj, state, layer, new_state=None, bb=2):
    n = proj.shape[0]
    hk = RET_HEADS * RET_DK
    hv = RET_HEADS * RET_DV
    _, xi, _, g_c = _ret_tables(1)
    qk = proj[:, :2 * hk]
    vg = proj[:, 2 * hk:]
    qkt = qk.reshape(n // bb, bb, 2 * hk).transpose(0, 2, 1)
    qk3 = qk.reshape(n // bb, bb, 2 * hk)
    vg3 = vg.reshape(n // bb, bb, 2 * hv)
    state_spec = pl.BlockSpec((None, bb, RET_HEADS, RET_DK, RET_DV), lambda i: (layer, i, 0, 0, 0))
    in_specs = [pl.BlockSpec((None, bb, 2 * hk), lambda i: (i, 0, 0)),
                pl.BlockSpec((None, bb, 2 * hv), lambda i: (i, 0, 0)),
                pl.BlockSpec((None, 2 * hk, bb), lambda i: (i, 0, 0)),
                state_spec,
                pl.BlockSpec((RET_HEADS, 1, 1), lambda i: (0, 0, 0)),
                pl.BlockSpec((RET_HEADS, 1, 1), lambda i: (0, 0, 0))]
    args = [qk3, vg3, qkt, state, xi, g_c]
    aliases = {}
    if new_state is not None:
        in_specs.append(pl.BlockSpec(memory_space=pl.ANY))
        args.append(new_state)
        aliases = {len(args) - 1: 1}
    o, s = pl.pallas_call(
        functools.partial(_ret_sample_kernel, bb),
        grid=(n // bb,),
        in_specs=in_specs,
        out_specs=[pl.BlockSpec((None, bb, hv), lambda i: (i, 0, 0)), state_spec],
        out_shape=[jax.ShapeDtypeStruct((n // bb, bb, hv), BF16),
                   jax.ShapeDtypeStruct(state.shape, F32)],
        input_output_aliases=aliases,
        compiler_params=_cparams("parallel"),
        name="ret_sample",
    )(*args)
    return o.reshape(n, hv), s


def _pool_prompt_kernel(tl, x_ref, halo_ref, g0_ref, w_ref, sc_ref, g1_ref, o_ref, last_ref, buf_ref):
    i = pl.program_id(1)
    x = x_ref[...]
    xn = _rms(x, g0_ref[...])
    hn = _rms(halo_ref[...], g0_ref[...])
    buf_ref[0:POOL_HALO, :] = jnp.where(i > 0, hn, 0.0)
    buf_ref[POOL_HALO:POOL_HALO + tl, :] = xn
    pos = i * tl + lax.broadcasted_iota(jnp.int32, (tl, 1), 0)
    ys = []
    for gi, w in enumerate(POOL_WINDOWS):
        cols = slice(gi * POOL_GW, (gi + 1) * POOL_GW)
        cur = xn[:, cols]
        acc = cur
        for j in range(1, w):
            acc = acc + buf_ref[POOL_HALO - j:POOL_HALO - j + tl, cols]
        cnt = jnp.minimum(pos + 1, w).astype(F32)
        z = acc / cnt - cur
        ys.append(jnp.dot(z.astype(BF16), w_ref[gi], preferred_element_type=F32))
    y = jnp.concatenate(ys, axis=1) * sc_ref[...]
    o_ref[...] = x + _rms(y, g1_ref[...])
    last_ref[...] = buf_ref[tl:tl + POOL_HALO, :]


def pool_prompt(x, g0, w_grp, scale, g1, tl=512):
    batch, seq, d = x.shape
    hb = tl // POOL_HALO
    vec = pl.BlockSpec((1, d), lambda b, i: (0, 0))
    return pl.pallas_call(
        functools.partial(_pool_prompt_kernel, tl),
        grid=(batch, seq // tl),
        in_specs=[pl.BlockSpec((None, tl, d), lambda b, i: (b, i, 0)),
                  pl.BlockSpec((None, POOL_HALO, d), lambda b, i: (b, jnp.maximum(i * hb - 1, 0), 0)),
                  vec,
                  pl.BlockSpec(w_grp.shape, lambda b, i: (0, 0, 0)),
                  vec, vec],
        out_specs=[pl.BlockSpec((None, tl, d), lambda b, i: (b, i, 0)),
                   pl.BlockSpec((None, POOL_HALO, d), lambda b, i: (b, 0, 0))],
        out_shape=[jax.ShapeDtypeStruct(x.shape, F32),
                   jax.ShapeDtypeStruct((batch, POOL_HALO, d), F32)],
        scratch_shapes=[pltpu.VMEM((POOL_HALO + tl, d), F32)],
        compiler_params=_cparams("parallel", "arbitrary"),
        name="pool_prompt",
    )(x, x, g0, w_grp, scale, g1)


def _pool_sample_kernel(x_ref, st_ref, g0_ref, w_ref, sc_ref, g1_ref, o_ref, nst_ref):
    d = D_MODEL
    x = x_ref[...]
    xn = _rms(x, g0_ref[...])
    ys = []
    for gi, w in enumerate(POOL_WINDOWS):
        acc = xn[:, gi * POOL_GW:(gi + 1) * POOL_GW]
        cur = acc
        for j in range(1, w):
            base = (POOL_PREV - j) * d + gi * POOL_GW
            acc = acc + st_ref[:, base:base + POOL_GW]
        cnt = float(min(PAST_LEN + 1, w))
        z = acc / cnt - cur
        ys.append(jnp.dot(z.astype(BF16), w_ref[gi], preferred_element_type=F32))
    y = jnp.concatenate(ys, axis=1) * sc_ref[...]
    o_ref[...] = x + _rms(y, g1_ref[...])
    nst_ref[:, :(POOL_PREV - 1) * d] = st_ref[:, d:]
    nst_ref[:, (POOL_PREV - 1) * d:] = xn


def pool_sample(x, state, g0, w_grp, scale, g1, tb=32):
    n, d = x.shape
    vec = pl.BlockSpec((1, d), lambda i: (0, 0))
    return pl.pallas_call(
        _pool_sample_kernel,
        grid=(n // tb,),
        in_specs=[pl.BlockSpec((tb, d), lambda i: (i, 0)),
                  pl.BlockSpec((tb, POOL_PREV * d), lambda i: (i, 0)),
                  vec,
                  pl.BlockSpec(w_grp.shape, lambda i: (0, 0, 0)),
                  vec, vec],
        out_specs=[pl.BlockSpec((tb, d), lambda i: (i, 0)),
                   pl.BlockSpec((tb, POOL_PREV * d), lambda i: (i, 0))],
        out_shape=[jax.ShapeDtypeStruct((n, d), F32),
                   jax.ShapeDtypeStruct(state.shape, F32)],
        compiler_params=_cparams("parallel"),
        name="pool_sample",
    )(x, state, g0, w_grp, scale, g1)


def _attn_proj_kernel(dil, kv_rows, nq, x_ref, g_ref, w_ref, perm_ref, kv_ref, xn_ref, res_ref):
    j = pl.program_id(1)
    seq, tn = perm_ref.shape
    ls = seq // dil

    @pl.when(j == 0)
    def _():
        xn_ref[...] = _rms(x_ref[...], g_ref[...]).astype(BF16)

    res = jnp.dot(xn_ref[...], w_ref[...], preferred_element_type=F32)
    if dil == 1:
        perm_ref[...] = res.astype(BF16)
    else:
        for c in range(tn // LANES):
            res_ref[c] = res[:, c * LANES:(c + 1) * LANES]
        for c in range(tn // LANES):
            for r in range(dil):
                perm_ref[r * ls:(r + 1) * ls, c * LANES:(c + 1) * LANES] = (
                    res_ref[c, pl.ds(r, ls, stride=dil), :].astype(BF16))

    @pl.when(j >= nq)
    def _():
        kv_ref[...] = res[seq - kv_rows:, :]


def attn_proj(x, g, w, gi, tn=256):
    batch, seq, d = x.shape
    win, dil = ATT_GROUPS[gi]
    kv_rows = min(win, seq)
    nq = ATT_HEADS * ATT_DH // tn
    ncol = ATT_GW // tn
    return pl.pallas_call(
        functools.partial(_attn_proj_kernel, dil, kv_rows, nq),
        grid=(batch, ncol),
        in_specs=[pl.BlockSpec((None, seq, d), lambda b, j: (b, 0, 0)),
                  pl.BlockSpec((1, d), lambda b, j: (0, 0)),
                  pl.BlockSpec((d, tn), lambda b, j: (0, gi * ncol + j))],
        out_specs=[pl.BlockSpec((None, seq, tn), lambda b, j: (b, 0, j)),
                   pl.BlockSpec((None, kv_rows, tn), lambda b, j: (b, 0, jnp.maximum(j - nq, 0)))],
        out_shape=[jax.ShapeDtypeStruct((batch, seq, ATT_GW), BF16),
                   jax.ShapeDtypeStruct((batch, kv_rows, ATT_GW - nq * tn), F32)],
        scratch_shapes=[pltpu.VMEM((seq, d), BF16), pltpu.VMEM((tn // LANES, seq, LANES), F32)],
        compiler_params=_cparams("parallel", "arbitrary"),
        name="attn_proj_g%d" % gi,
    )(x, g, w)


def _alibi_slope(group, head):
    n = len(ATT_GROUPS) * ATT_HEADS
    idx = jnp.full((ATT_BLK, ATT_BLK), group * ATT_HEADS + 1, F32) + head
    return jnp.exp2(-8.0 * idx / n)


def _attn_prompt_kernel(seq, *refs):
    ng = len(ATT_GROUPS)
    qkv = [refs[3 * g:3 * g + 3] for g in range(ng)]
    out_ref, o_s, m_s, l_s = refs[3 * ng:]
    hp = pl.program_id(1)
    lane = lax.broadcasted_iota(jnp.int32, (ATT_BLK, LANES), 1)
    first_head = lane < ATT_DH
    qi = lax.broadcasted_iota(jnp.int32, (ATT_BLK, ATT_BLK), 0)
    kj = lax.broadcasted_iota(jnp.int32, (ATT_BLK, ATT_BLK), 1)
    scale = ATT_DH ** -0.5
    nt = (((1,), (1,)), ((), ()))
    ones = jnp.ones((ATT_BLK, LANES), BF16)

    for g, (win, dil) in enumerate(ATT_GROUPS):
        q_ref, k_ref, v_ref = qkv[g]
        n_keys = win // dil
        ls = seq // dil
        nb = ls // ATT_BLK
        d_cur = qi - kj
        d_prev = qi + ATT_BLK - kj
        bias_cur, bias_prev = [], []
        for hh in range(2):
            slope = _alibi_slope(g, (hp * 2 + hh).astype(F32))
            bias_cur.append(jnp.where(d_cur >= 0, -slope * (dil * d_cur).astype(F32), NEG_INF))
            bias_prev.append(jnp.where(d_prev <= n_keys, -slope * (dil * d_prev).astype(F32), NEG_INF))
        bias_cur = jnp.concatenate(bias_cur, axis=0)
        bias_prev = jnp.concatenate(bias_prev, axis=0)
        rows = lambda qb: slice(qb * ATT_BLK, (qb + 1) * ATT_BLK)

        for c0 in range(0, seq // ATT_BLK, ATT_INTERLEAVE):
            blocks = range(c0, c0 + ATT_INTERLEAVE)
            prev_blocks = [qb for qb in blocks if qb % nb > 0]
            qm = {}
            for qb in blocks:
                qf = q_ref[rows(qb), :].astype(F32) * scale
                qm[qb] = jnp.concatenate([jnp.where(first_head, qf, 0.0), jnp.where(first_head, 0.0, qf)],
                                         axis=0).astype(BF16)
            s_cur = {qb: lax.dot_general(qm[qb], k_ref[rows(qb), :], nt, preferred_element_type=F32) + bias_cur
                     for qb in blocks}
            s_prev = {qb: lax.dot_general(qm[qb], k_ref[rows(qb - 1), :], nt, preferred_element_type=F32)
                      + bias_prev for qb in prev_blocks}
            halves = (slice(0, ATT_BLK), slice(ATT_BLK, 2 * ATT_BLK))
            m = {}
            for qb in blocks:
                for hh, half in enumerate(halves):
                    m[qb, hh] = jnp.max(s_cur[qb][half], axis=-1, keepdims=True)
                    if qb in s_prev:
                        m[qb, hh] = jnp.maximum(m[qb, hh], jnp.max(s_prev[qb][half], axis=-1, keepdims=True))
            probs = lambda s, qb: jnp.concatenate(
                [jnp.exp(s[half] - m[qb, hh]) for hh, half in enumerate(halves)], axis=0).astype(BF16)
            p_cur = {qb: probs(s_cur[qb], qb) for qb in blocks}
            p_prev = {qb: probs(s_prev[qb], qb) for qb in prev_blocks}
            v1 = {qb: jnp.concatenate([v_ref[rows(qb), :], ones], axis=1)
                  for qb in sorted(set(blocks) | {qb - 1 for qb in prev_blocks})}
            for qb in blocks:
                acc = jnp.dot(p_cur[qb], v1[qb], preferred_element_type=F32)
                if qb in p_prev:
                    acc = acc + jnp.dot(p_prev[qb], v1[qb - 1], preferred_element_type=F32)
                blk, res = qb % nb, qb // nb
                dst = pl.ds(blk * ATT_BLK * dil + res, ATT_BLK, stride=dil) if dil > 1 else rows(qb)
                o_s[g, dst, :] = jnp.where(first_head, acc[:ATT_BLK, :LANES], acc[ATT_BLK:, :LANES])
                l_s[g, dst, :] = jnp.where(first_head, acc[:ATT_BLK, LANES:], acc[ATT_BLK:, LANES:])
                m_s[g, dst, :] = jnp.where(first_head, m[qb, 0], m[qb, 1])

    m_all = m_s[0]
    for g in range(1, ng):
        m_all = jnp.maximum(m_all, m_s[g])
    num = jnp.zeros((seq, LANES), F32)
    den = jnp.zeros((seq, LANES), F32)
    for g in range(ng):
        e = jnp.exp(m_s[g] - m_all)
        num = num + e * o_s[g]
        den = den + e * l_s[g]
    out_ref[...] = (num / den).astype(out_ref.dtype)


def attn_prompt(qkv_groups, batch, seq):
    ng = len(ATT_GROUPS)
    hpairs = ATT_HEADS * ATT_DH // LANES
    in_specs, args = [], []
    for g in range(ng):
        for part in range(3):
            in_specs.append(pl.BlockSpec((seq, LANES), lambda b, h, part=part: (b, part * hpairs + h)))
            args.append(qkv_groups[g])
    return pl.pallas_call(
        functools.partial(_attn_prompt_kernel, seq),
        grid=(batch, hpairs),
        in_specs=in_specs,
        out_specs=pl.BlockSpec((seq, LANES), lambda b, h: (b, h)),
        out_shape=jax.ShapeDtypeStruct((batch * seq, D_MODEL), BF16),
        scratch_shapes=[pltpu.VMEM((ng, seq, LANES), F32)] * 3,
        compiler_params=_cparams("parallel", "parallel"),
        name="attn_prompt",
    )(*args)


def _attn_sample_kernel(hb, qkv_ref, slope_ref, c0_ref, c1_ref, c2_ref, o_ref):
    hg = pl.program_id(0)
    b = pl.program_id(1)
    caches = (c0_ref, c1_ref, c2_ref)
    ng = len(ATT_GROUPS)
    scale = ATT_DH ** -0.5

    @pl.when(b == 0)
    def _():
        o_ref[...] = jnp.zeros_like(o_ref)

    this_seq = lax.broadcasted_iota(jnp.int32, (ATT_DH, LANES), 1) == b

    for hh in range(hb):
        rows = slice(hh * ATT_DH, (hh + 1) * ATT_DH)
        cols = [[jnp.sum(jnp.where(this_seq, qkv_ref[g, part, rows, :], 0.0), axis=1, keepdims=True)
                 for part in range(3)] for g in range(ng)]
        scores, new_scores = [], []
        m = None
        for g, (_, dil) in enumerate(ATT_GROUPS):
            wb = caches[g].shape[-1]
            qc = cols[g][0] * scale
            s = jnp.sum(caches[g][0, hh] * qc, axis=0, keepdims=True)
            pos = lax.broadcasted_iota(jnp.int32, (1, wb), 1)
            slope = slope_ref[g, hg * hb + hh]
            s = jnp.where((pos & (dil - 1)) == 0, s - slope * (wb - pos).astype(F32), NEG_INF)
            s_new = jnp.sum(qc * cols[g][1], axis=0, keepdims=True)
            mg = jnp.maximum(jnp.max(s, axis=1, keepdims=True), s_new)
            m = mg if m is None else jnp.maximum(m, mg)
            scores.append(s)
            new_scores.append(s_new)
        l = jnp.zeros((1, 1), F32)
        o = jnp.zeros((ATT_DH, 1), F32)
        for g in range(ng):
            p = jnp.exp(scores[g] - m)
            p_new = jnp.exp(new_scores[g] - m)
            l = l + jnp.sum(p, axis=1, keepdims=True) + p_new
            o = o + jnp.sum(caches[g][1, hh] * p, axis=1, keepdims=True) + p_new * cols[g][2]
        o_ref[rows, :] = jnp.where(this_seq, o / l, o_ref[rows, :])


def attn_sample(proj, caches, layer, hb=4):
    n = proj.shape[0]
    ng = len(ATT_GROUPS)
    hd = ATT_HEADS * ATT_DH
    assert n == LANES
    for c, (win, dil) in zip(caches, ATT_GROUPS):
        assert c.shape[2] == win and win % dil == 0 and dil & (dil - 1) == 0
    nall = ng * ATT_HEADS
    slopes = jnp.exp2(-8.0 * jnp.arange(1, nall + 1, dtype=F32) / nall).reshape(ng, ATT_HEADS)
    qkv_t = proj.T.reshape(ng, 3, hd, n)
    cache_t = [jnp.transpose(c, (0, 1, 3, 4, 5, 2)) for c in caches]
    cache_specs = [pl.BlockSpec((None, None, 2, hb, ATT_DH, c.shape[-1]), lambda h, b: (layer, b, 0, h, 0, 0))
                   for c in cache_t]
    out_t = pl.pallas_call(
        functools.partial(_attn_sample_kernel, hb),
        grid=(ATT_HEADS // hb, n),
        in_specs=[pl.BlockSpec((ng, 3, hb * ATT_DH, n), lambda h, b: (0, 0, h, 0)),
                  pl.BlockSpec(memory_space=pltpu.SMEM)] + cache_specs,
        out_specs=pl.BlockSpec((hb * ATT_DH, n), lambda h, b: (h, 0)),
        out_shape=jax.ShapeDtypeStruct((hd, n), F32),
        compiler_params=_cparams("parallel", "arbitrary"),
        name="attn_sample",
    )(qkv_t, slopes, *cache_t)
    return out_t.T


def kernel(x_prompt, x_sample, state_ret, state_pool, cache_kv_g0, cache_kv_g1, cache_kv_g2, norm_gains,
           w_in_ret, w_out_ret, w_grp_pool, scale_pool, w_in_attn, w_out_attn, w_ffn_in, w_ffn_out):
    batch, seq, d = x_prompt.shape
    ns = x_sample.shape[0]
    caches = (cache_kv_g0, cache_kv_g1, cache_kv_g2)
    ng = len(ATT_GROUPS)
    xp = x_prompt.reshape(batch * seq, d)
    xs = x_sample.reshape(ns, d)
    ret_p, pool_p, pool_s = [], [], []
    ret_s = None
    kv_p = [[] for _ in range(ng)]
    kv_s = [[] for _ in range(ng)]

    for i in range(DEPTH):
        kind, j = i % N_MIXERS, i // N_MIXERS
        g = [norm_gains[i, r][None, :] for r in range(4)]
        if kind == 0:
            w_in = w_in_ret[j].astype(BF16)
            w_out = w_out_ret[j].astype(BF16)
            proj = norm_matmul(xp, g[0], w_in, BF16, 1024, 1024)
            o, sp = ret_prompt(proj, batch, seq)
            xp = matmul_norm_res(o, w_out, g[1], xp, 512)
            proj_s = norm_matmul(xs, g[0], w_in, F32, ns, 1024)
            o_s, ret_s = ret_sample(proj_s, state_ret, j, ret_s)
            xs = matmul_norm_res(o_s, w_out, g[1], xs, ns)
            ret_p.append(sp)
        elif kind == 1:
            w_grp = w_grp_pool[j].astype(BF16)
            scale = scale_pool[j][None, :]
            xp3, last = pool_prompt(xp.reshape(batch, seq, d), g[0], w_grp, scale, g[1])
            xp = xp3.reshape(batch * seq, d)
            xs, nst = pool_sample(xs, state_pool[j].reshape(ns, POOL_PREV * d), g[0], w_grp, scale, g[1])
            pool_p.append(last[:, POOL_HALO - POOL_PREV:])
            pool_s.append(nst.reshape(ns, POOL_PREV, d))
        else:
            w_in = w_in_attn[j].astype(BF16)
            w_out = w_out_attn[j].astype(BF16)
            x3 = xp.reshape(batch, seq, d)
            qkv = []
            for gi in range(ng):
                perm, kv = attn_proj(x3, g[0], w_in, gi)
                qkv.append(perm.reshape(batch * seq, ATT_GW))
                kv_p[gi].append(kv.reshape(batch, kv.shape[1], 2, ATT_HEADS, ATT_DH))
            a = attn_prompt(qkv, batch, seq)
            xp = matmul_norm_res(a, w_out, g[1], xp, 512)
            proj_s = norm_matmul(xs, g[0], w_in, F32, ns, 1024)
            a_s = attn_sample(proj_s, caches, j)
            xs = matmul_norm_res(a_s.astype(BF16), w_out, g[1], xs, ns)
            p5 = proj_s.reshape(ns, 1, ng, 3, ATT_HEADS, ATT_DH)
            for gi in range(ng):
                kv_s[gi].append(p5[:, :, gi, 1:])
        w_fi = w_ffn_in[i].astype(BF16)
        w_fo = w_ffn_out[i].astype(BF16)
        xp = ffn(xp, g[2], w_fi, w_fo, g[3], 1024, 256)
        xs = ffn(xs, g[2], w_fi, w_fo, g[3], ns, 256)

    return (xp.reshape(batch, seq, d), xs.reshape(ns, 1, d),
            jnp.stack(ret_p), ret_s,
            jnp.stack(pool_p), jnp.stack(pool_s),
            jnp.stack(kv_p[0]), jnp.stack(kv_s[0]),
            jnp.stack(kv_p[1]), jnp.stack(kv_s[1]),
            jnp.stack(kv_p[2]), jnp.stack(kv_s[2]))
```

```python
import functools

import jax
import jax.numpy as jnp
from jax import lax
from jax.experimental import pallas as pl
from jax.experimental.pallas import tpu as pltpu

F32 = jnp.float32
BF16 = jnp.bfloat16

D_MODEL = 1024
DEPTH = 4
PAST_LEN = 2048
N_MIXERS = 3

RET_HEADS = 4
RET_DK = D_MODEL // RET_HEADS
RET_DV = 2 * D_MODEL // RET_HEADS
RET_CHUNK = 128

POOL_WINDOWS = (2, 4, 8, 16)
POOL_GW = D_MODEL // len(POOL_WINDOWS)
POOL_PREV = max(POOL_WINDOWS) - 1
POOL_HALO = POOL_PREV + 1

ATT_GROUPS = ((128, 1), (512, 4), (2048, 16))
ATT_HEADS = 16
ATT_DH = D_MODEL // ATT_HEADS
ATT_GW = 3 * ATT_HEADS * ATT_DH
ATT_BLK = 128
ATT_INTERLEAVE = 4

FFN_HIDDEN = 2816

NORM_EPS = 1e-6
GN_EPS = 1e-5
NEG_INF = -1e30

LANES = 128
VMEM_LIMIT = 56 * 1024 * 1024


def _cparams(*sem):
    return pltpu.CompilerParams(dimension_semantics=sem, vmem_limit_bytes=VMEM_LIMIT)


def _rms(x, g):
    return x * lax.rsqrt(jnp.mean(x * x, axis=-1, keepdims=True) + NORM_EPS) * g


def _silu(x):
    return x * jax.nn.sigmoid(x)


def _resident(shape):
    return pl.BlockSpec(shape, lambda *_: (0,) * len(shape), pipeline_mode=pl.Buffered(1))


def _layer_resident(w, layer):
    return pl.BlockSpec((None,) + w.shape[1:], lambda *_: (layer,) + (0,) * (w.ndim - 1),
                        pipeline_mode=pl.Buffered(1))


def _norm_matmul_kernel(tn, x_ref, g_ref, w_ref, o_ref):
    xn = _rms(x_ref[...], g_ref[...]).astype(BF16)
    for j in range(w_ref.shape[1] // tn):
        cols = slice(j * tn, (j + 1) * tn)
        o_ref[:, cols] = jnp.dot(xn, w_ref[:, cols], preferred_element_type=F32).astype(o_ref.dtype)


def norm_matmul(x, g, w, layer, out_dtype, tm, tn=512):
    m, d = x.shape
    n = w.shape[2]
    return pl.pallas_call(
        functools.partial(_norm_matmul_kernel, tn),
        grid=(m // tm,),
        in_specs=[pl.BlockSpec((tm, d), lambda i: (i, 0)), _resident((1, d)), _layer_resident(w, layer)],
        out_specs=pl.BlockSpec((tm, n), lambda i: (i, 0)),
        out_shape=jax.ShapeDtypeStruct((m, n), out_dtype),
        compiler_params=_cparams("parallel"),
        name="norm_matmul",
    )(x, g, w)


def _matmul_norm_res_kernel(a_ref, w_ref, g_ref, x_ref, o_ref):
    y = jnp.dot(a_ref[...], w_ref[...], preferred_element_type=F32)
    o_ref[...] = x_ref[...] + _rms(y, g_ref[...])


def matmul_norm_res(a, w, layer, g, x, tm):
    m, k = a.shape
    d = w.shape[2]
    return pl.pallas_call(
        _matmul_norm_res_kernel,
        grid=(m // tm,),
        in_specs=[pl.BlockSpec((tm, k), lambda i: (i, 0)),
                  _layer_resident(w, layer),
                  _resident((1, d)),
                  pl.BlockSpec((tm, d), lambda i: (i, 0))],
        out_specs=pl.BlockSpec((tm, d), lambda i: (i, 0)),
        out_shape=jax.ShapeDtypeStruct((m, d), F32),
        compiler_params=_cparams("parallel"),
        name="matmul_norm_res",
    )(a, w, g, x)


def _ffn_kernel(th, x_ref, g2_ref, wi_ref, wo_ref, g3_ref, o_ref, a_ref):
    x = x_ref[...]
    xn = _rms(x, g2_ref[...]).astype(BF16)
    for j in range(FFN_HIDDEN // th):
        gate = jnp.dot(xn, wi_ref[:, j * th:(j + 1) * th], preferred_element_type=F32)
        up = jnp.dot(xn, wi_ref[:, FFN_HIDDEN + j * th:FFN_HIDDEN + (j + 1) * th], preferred_element_type=F32)
        a_ref[:, j * th:(j + 1) * th] = (_silu(gate) * up).astype(BF16)
    y = jnp.dot(a_ref[...], wo_ref[...], preferred_element_type=F32)
    o_ref[...] = x + _rms(y, g3_ref[...])


def ffn(x, g2, w_in, w_out, layer, g3, tm, th=256):
    m, d = x.shape
    return pl.pallas_call(
        functools.partial(_ffn_kernel, th),
        grid=(m // tm,),
        in_specs=[pl.BlockSpec((tm, d), lambda i: (i, 0)),
                  _resident((1, d)), _layer_resident(w_in, layer), _layer_resident(w_out, layer),
                  _resident((1, d))],
        out_specs=pl.BlockSpec((tm, d), lambda i: (i, 0)),
        out_shape=jax.ShapeDtypeStruct((m, d), F32),
        scratch_shapes=[pltpu.VMEM((tm, FFN_HIDDEN), BF16)],
        compiler_params=_cparams("parallel"),
        name="ffn",
    )(x, g2, w_in, w_out, g3)


def _ret_tables(c):
    lg = jnp.log1p(-jnp.exp2(-5.0 - jnp.arange(RET_HEADS, dtype=F32)))
    n = jnp.arange(c, dtype=F32)
    diff = n[:, None] - n[None, :]
    decay = jnp.where(diff[None] >= 0, jnp.exp(jnp.maximum(diff, 0.0)[None] * lg[:, None, None]), 0.0)
    xi = jnp.exp((n[:, None] + 1.0) * lg[None, :])
    zeta = jnp.exp((c - 1.0 - n)[:, None] * lg[None, :])
    g_c = jnp.exp(c * lg)
    return decay, xi.T[:, :, None], zeta.T[:, :, None], g_c[:, None, None]


def _group_norm_gate(o, gate):
    mu = jnp.mean(o, axis=-1, keepdims=True)
    oc = o - mu
    var = jnp.mean(oc * oc, axis=-1, keepdims=True)
    return oc * lax.rsqrt(var + GN_EPS) * _silu(gate)


def _ret_prompt_kernel(nsub, p_ref, decay_ref, xi_ref, zeta_ref, gc_ref, o_ref, s_ref):
    @pl.when(pl.program_id(1) == 0)
    def _():
        s_ref[...] = jnp.zeros_like(s_ref)

    c = RET_CHUNK
    hk = RET_HEADS * RET_DK
    hv = RET_HEADS * RET_DV
    nt = (((1,), (1,)), ((), ()))
    for h in range(RET_HEADS):
        s = s_ref[h]
        for cc in range(nsub):
            rows = slice(cc * c, (cc + 1) * c)
            q = p_ref[rows, h * RET_DK:(h + 1) * RET_DK] * (RET_DK ** -0.5)
            k = p_ref[rows, hk + h * RET_DK:hk + (h + 1) * RET_DK]
            v = p_ref[rows, 2 * hk + h * RET_DV:2 * hk + (h + 1) * RET_DV]
            gt = p_ref[rows, 2 * hk + hv + h * RET_DV:2 * hk + hv + (h + 1) * RET_DV]
            a = lax.dot_general(q, k, nt, preferred_element_type=F32) * decay_ref[h]
            o = (jnp.dot(a.astype(BF16), v, preferred_element_type=F32)
                 + jnp.dot(q, s.astype(BF16), preferred_element_type=F32) * xi_ref[h])
            kz = (k.astype(F32) * zeta_ref[h]).T.astype(BF16)
            s = gc_ref[h] * s + jnp.dot(kz, v, preferred_element_type=F32)
            o_ref[rows, h * RET_DV:(h + 1) * RET_DV] = _group_norm_gate(o, gt.astype(F32)).astype(o_ref.dtype)
        s_ref[h] = s


def ret_prompt(proj, batch, seq, nsub=2):
    rows = nsub * RET_CHUNK
    nstep = seq // rows
    decay, xi, zeta, g_c = _ret_tables(RET_CHUNK)
    full = lambda a: pl.BlockSpec(a.shape, lambda b, i: (0,) * a.ndim)
    return pl.pallas_call(
        functools.partial(_ret_prompt_kernel, nsub),
        grid=(batch, nstep),
        in_specs=[pl.BlockSpec((rows, proj.shape[1]), lambda b, i: (b * nstep + i, 0)),
                  full(decay), full(xi), full(zeta), full(g_c)],
        out_specs=[pl.BlockSpec((rows, RET_HEADS * RET_DV), lambda b, i: (b * nstep + i, 0)),
                   pl.BlockSpec((None, RET_HEADS, RET_DK, RET_DV), lambda b, i: (b, 0, 0, 0))],
        out_shape=[jax.ShapeDtypeStruct((batch * seq, RET_HEADS * RET_DV), BF16),
                   jax.ShapeDtypeStruct((batch, RET_HEADS, RET_DK, RET_DV), F32)],
        compiler_params=_cparams("parallel", "arbitrary"),
        name="ret_prompt",
    )(proj, decay, xi, zeta, g_c)


def _ret_sample_kernel(bb, qk_ref, vg_ref, qkt_ref, s0_ref, xi_ref, gc_ref, *rest):
    o_ref, s_ref = rest[-2:]
    hk = RET_HEADS * RET_DK
    hv = RET_HEADS * RET_DV
    scale = RET_DK ** -0.5
    for i in range(bb):
        for h in range(RET_HEADS):
            q_row = qk_ref[i:i + 1, h * RET_DK:(h + 1) * RET_DK] * scale
            k_row = qk_ref[i:i + 1, hk + h * RET_DK:hk + (h + 1) * RET_DK]
            v_row = vg_ref[i:i + 1, h * RET_DV:(h + 1) * RET_DV]
            g_row = vg_ref[i:i + 1, hv + h * RET_DV:hv + (h + 1) * RET_DV]
            q_col = qkt_ref[h * RET_DK:(h + 1) * RET_DK, i:i + 1] * scale
            k_col = qkt_ref[hk + h * RET_DK:hk + (h + 1) * RET_DK, i:i + 1]
            s0 = s0_ref[i, h]
            a = jnp.sum(q_row * k_row, axis=-1, keepdims=True)
            qs = jnp.sum(q_col * s0, axis=0, keepdims=True)
            o = a * v_row + qs * xi_ref[h]
            s_ref[i, h] = gc_ref[h] * s0 + k_col * v_row
            o_ref[i:i + 1, h * RET_DV:(h + 1) * RET_DV] = _group_norm_gate(o, g_row).astype(o_ref.dtype)


def ret_sample(proj, state, layer, new_state=None, bb=2):
    n = proj.shape[0]
    hk = RET_HEADS * RET_DK
    hv = RET_HEADS * RET_DV
    _, xi, _, g_c = _ret_tables(1)
    qk = proj[:, :2 * hk]
    vg = proj[:, 2 * hk:]
    qkt = qk.reshape(n // bb, bb, 2 * hk).transpose(0, 2, 1)
    qk3 = qk.reshape(n // bb, bb, 2 * hk)
    vg3 = vg.reshape(n // bb, bb, 2 * hv)
    state_spec = pl.BlockSpec((None, bb, RET_HEADS, RET_DK, RET_DV), lambda i: (layer, i, 0, 0, 0))
    in_specs = [pl.BlockSpec((None, bb, 2 * hk), lambda i: (i, 0, 0)),
                pl.BlockSpec((None, bb, 2 * hv), lambda i: (i, 0, 0)),
                pl.BlockSpec((None, 2 * hk, bb), lambda i: (i, 0, 0)),
                state_spec,
                pl.BlockSpec((RET_HEADS, 1, 1), lambda i: (0, 0, 0)),
                pl.BlockSpec((RET_HEADS, 1, 1), lambda i: (0, 0, 0))]
    args = [qk3, vg3, qkt, state, xi, g_c]
    aliases = {}
    if new_state is not None:
        in_specs.append(pl.BlockSpec(memory_space=pl.ANY))
        args.append(new_state)
        aliases = {len(args) - 1: 1}
    o, s = pl.pallas_call(
        functools.partial(_ret_sample_kernel, bb),
        grid=(n // bb,),
        in_specs=in_specs,
        out_specs=[pl.BlockSpec((None, bb, hv), lambda i: (i, 0, 0)), state_spec],
        out_shape=[jax.ShapeDtypeStruct((n // bb, bb, hv), BF16),
                   jax.ShapeDtypeStruct(state.shape, F32)],
        input_output_aliases=aliases,
        compiler_params=_cparams("parallel"),
        name="ret_sample",
    )(*args)
    return o.reshape(n, hv), s


def _pool_prompt_kernel(tl, x_ref, halo_ref, g0_ref, w_ref, sc_ref, g1_ref, o_ref, last_ref, buf_ref):
    i = pl.program_id(1)
    x = x_ref[...]
    xn = _rms(x, g0_ref[...])
    hn = _rms(halo_ref[...], g0_ref[...])
    buf_ref[0:POOL_HALO, :] = jnp.where(i > 0, hn, 0.0)
    buf_ref[POOL_HALO:POOL_HALO + tl, :] = xn
    pos = i * tl + lax.broadcasted_iota(jnp.int32, (tl, 1), 0)
    ys = []
    for gi, w in enumerate(POOL_WINDOWS):
        cols = slice(gi * POOL_GW, (gi + 1) * POOL_GW)
        cur = xn[:, cols]
        acc = cur
        for j in range(1, w):
            acc = acc + buf_ref[POOL_HALO - j:POOL_HALO - j + tl, cols]
        cnt = jnp.minimum(pos + 1, w).astype(F32)
        z = acc / cnt - cur
        ys.append(jnp.dot(z.astype(BF16), w_ref[gi], preferred_element_type=F32))
    y = jnp.concatenate(ys, axis=1) * sc_ref[...]
    o_ref[...] = x + _rms(y, g1_ref[...])
    last_ref[...] = buf_ref[tl:tl + POOL_HALO, :]


def pool_prompt(x, g0, w_grp, layer, scale, g1, tl=512):
    batch, seq, d = x.shape
    hb = tl // POOL_HALO
    vec = pl.BlockSpec((1, d), lambda b, i: (0, 0))
    return pl.pallas_call(
        functools.partial(_pool_prompt_kernel, tl),
        grid=(batch, seq // tl),
        in_specs=[pl.BlockSpec((None, tl, d), lambda b, i: (b, i, 0)),
                  pl.BlockSpec((None, POOL_HALO, d), lambda b, i: (b, jnp.maximum(i * hb - 1, 0), 0)),
                  vec,
                  _layer_resident(w_grp, layer),
                  vec, vec],
        out_specs=[pl.BlockSpec((None, tl, d), lambda b, i: (b, i, 0)),
                   pl.BlockSpec((None, POOL_HALO, d), lambda b, i: (b, 0, 0))],
        out_shape=[jax.ShapeDtypeStruct(x.shape, F32),
                   jax.ShapeDtypeStruct((batch, POOL_HALO, d), F32)],
        scratch_shapes=[pltpu.VMEM((POOL_HALO + tl, d), F32)],
        compiler_params=_cparams("parallel", "arbitrary"),
        name="pool_prompt",
    )(x, x, g0, w_grp, scale, g1)


def _pool_sample_kernel(x_ref, st_ref, g0_ref, w_ref, sc_ref, g1_ref, o_ref, nst_ref):
    d = D_MODEL
    x = x_ref[...]
    xn = _rms(x, g0_ref[...])
    ys = []
    for gi, w in enumerate(POOL_WINDOWS):
        acc = xn[:, gi * POOL_GW:(gi + 1) * POOL_GW]
        cur = acc
        for j in range(1, w):
            base = (POOL_PREV - j) * d + gi * POOL_GW
            acc = acc + st_ref[:, base:base + POOL_GW]
        cnt = float(min(PAST_LEN + 1, w))
        z = acc / cnt - cur
        ys.append(jnp.dot(z.astype(BF16), w_ref[gi], preferred_element_type=F32))
    y = jnp.concatenate(ys, axis=1) * sc_ref[...]
    o_ref[...] = x + _rms(y, g1_ref[...])
    nst_ref[:, :(POOL_PREV - 1) * d] = st_ref[:, d:]
    nst_ref[:, (POOL_PREV - 1) * d:] = xn


def pool_sample(x, state, g0, w_grp, layer, scale, g1, tb=32):
    n, d = x.shape
    vec = pl.BlockSpec((1, d), lambda i: (0, 0))
    return pl.pallas_call(
        _pool_sample_kernel,
        grid=(n // tb,),
        in_specs=[pl.BlockSpec((tb, d), lambda i: (i, 0)),
                  pl.BlockSpec((tb, POOL_PREV * d), lambda i: (i, 0)),
                  vec,
                  _layer_resident(w_grp, layer),
                  vec, vec],
        out_specs=[pl.BlockSpec((tb, d), lambda i: (i, 0)),
                   pl.BlockSpec((tb, POOL_PREV * d), lambda i: (i, 0))],
        out_shape=[jax.ShapeDtypeStruct((n, d), F32),
                   jax.ShapeDtypeStruct(state.shape, F32)],
        compiler_params=_cparams("parallel"),
        name="pool_sample",
    )(x, state, g0, w_grp, scale, g1)


def _attn_proj_kernel(dil, tn, x_ref, g_ref, w_ref, perm_ref, kv_ref, res_ref):
    rows, kv_cols = kv_ref.shape
    t = x_ref.shape[0]
    nq = w_ref.shape[1] - kv_cols
    xn = _rms(x_ref[...], g_ref[...]).astype(BF16)
    for j in range(w_ref.shape[1] // tn):
        cols = slice(j * tn, (j + 1) * tn)
        res = jnp.dot(xn, w_ref[:, cols], preferred_element_type=F32)
        if dil == 1:
            perm_ref[0, :, cols] = res.astype(BF16)
        else:
            for c in range(tn // LANES):
                slot = (j * (tn // LANES) + c) % res_ref.shape[0]
                res_ref[slot] = res[:, c * LANES:(c + 1) * LANES]
                for r in range(dil):
                    perm_ref[r, :, j * tn + c * LANES:j * tn + (c + 1) * LANES] = (
                        res_ref[slot, pl.ds(r, t // dil, stride=dil), :].astype(BF16))
        if j * tn >= nq:
            kv_ref[:, j * tn - nq:(j + 1) * tn - nq] = res[t - rows:, :]


def attn_proj(x, g, w, layer, gi, t=512, tn=256):
    batch, seq, d = x.shape
    win, dil = ATT_GROUPS[gi]
    kv_rows = min(win, seq)
    nq = ATT_HEADS * ATT_DH
    tkv = min(t, kv_rows)
    first = (seq - kv_rows) // t
    kv_index = lambda b, i: (b, jnp.maximum(i - first, 0), 0)
    perm, kv = pl.pallas_call(
        functools.partial(_attn_proj_kernel, dil, tn),
        grid=(batch, seq // t),
        in_specs=[pl.BlockSpec((None, t, d), lambda b, i: (b, i, 0)),
                  _resident((1, d)),
                  pl.BlockSpec((None, d, ATT_GW), lambda b, i: (layer, 0, gi), pipeline_mode=pl.Buffered(1))],
        out_specs=[pl.BlockSpec((None, dil, t // dil, ATT_GW), lambda b, i: (b, 0, i, 0)),
                   pl.BlockSpec((None, tkv, ATT_GW - nq), kv_index)],
        out_shape=[jax.ShapeDtypeStruct((batch, dil, seq // dil, ATT_GW), BF16),
                   jax.ShapeDtypeStruct((batch, kv_rows, ATT_GW - nq), F32)],
        scratch_shapes=[pltpu.VMEM((2 * tn // LANES, t, LANES), F32)],
        compiler_params=_cparams("parallel", "arbitrary"),
        name="attn_proj_g%d" % gi,
    )(x, g, w)
    return perm.reshape(batch, seq, ATT_GW), kv


def _alibi_slope(group, head):
    n = len(ATT_GROUPS) * ATT_HEADS
    idx = jnp.full((ATT_BLK, ATT_BLK), group * ATT_HEADS + 1, F32) + head
    return jnp.exp2(-8.0 * idx / n)


def _attn_prompt_kernel(seq, *refs):
    ng = len(ATT_GROUPS)
    qkv = [refs[3 * g:3 * g + 3] for g in range(ng)]
    out_ref, o_s, m_s, l_s = refs[3 * ng:]
    hp = pl.program_id(1)
    lane = lax.broadcasted_iota(jnp.int32, (ATT_BLK, LANES), 1)
    first_head = lane < ATT_DH
    qi = lax.broadcasted_iota(jnp.int32, (ATT_BLK, ATT_BLK), 0)
    kj = lax.broadcasted_iota(jnp.int32, (ATT_BLK, ATT_BLK), 1)
    scale = ATT_DH ** -0.5
    nt = (((1,), (1,)), ((), ()))
    ones = jnp.ones((ATT_BLK, LANES), BF16)

    for g, (win, dil) in enumerate(ATT_GROUPS):
        q_ref, k_ref, v_ref = qkv[g]
        n_keys = win // dil
        ls = seq // dil
        nb = ls // ATT_BLK
        d_cur = qi - kj
        d_prev = qi + ATT_BLK - kj
        bias_cur, bias_prev = [], []
        for hh in range(2):
            slope = _alibi_slope(g, (hp * 2 + hh).astype(F32))
            bias_cur.append(jnp.where(d_cur >= 0, -slope * (dil * d_cur).astype(F32), NEG_INF))
            bias_prev.append(jnp.where(d_prev <= n_keys, -slope * (dil * d_prev).astype(F32), NEG_INF))
        bias_cur = jnp.concatenate(bias_cur, axis=0)
        bias_prev = jnp.concatenate(bias_prev, axis=0)
        rows = lambda qb: slice(qb * ATT_BLK, (qb + 1) * ATT_BLK)

        for c0 in range(0, seq // ATT_BLK, ATT_INTERLEAVE):
            blocks = range(c0, c0 + ATT_INTERLEAVE)
            prev_blocks = [qb for qb in blocks if qb % nb > 0]
            qm = {}
            for qb in blocks:
                qf = q_ref[rows(qb), :].astype(F32) * scale
                qm[qb] = jnp.concatenate([jnp.where(first_head, qf, 0.0), jnp.where(first_head, 0.0, qf)],
                                         axis=0).astype(BF16)
            s_cur = {qb: lax.dot_general(qm[qb], k_ref[rows(qb), :], nt, preferred_element_type=F32) + bias_cur
                     for qb in blocks}
            s_prev = {qb: lax.dot_general(qm[qb], k_ref[rows(qb - 1), :], nt, preferred_element_type=F32)
                      + bias_prev for qb in prev_blocks}
            halves = (slice(0, ATT_BLK), slice(ATT_BLK, 2 * ATT_BLK))
            m = {}
            for qb in blocks:
                for hh, half in enumerate(halves):
                    m[qb, hh] = jnp.max(s_cur[qb][half], axis=-1, keepdims=True)
                    if qb in s_prev:
                        m[qb, hh] = jnp.maximum(m[qb, hh], jnp.max(s_prev[qb][half], axis=-1, keepdims=True))
            probs = lambda s, qb: jnp.concatenate(
                [jnp.exp(s[half] - m[qb, hh]) for hh, half in enumerate(halves)], axis=0).astype(BF16)
            p_cur = {qb: probs(s_cur[qb], qb) for qb in blocks}
            p_prev = {qb: probs(s_prev[qb], qb) for qb in prev_blocks}
            v1 = {qb: jnp.concatenate([v_ref[rows(qb), :], ones], axis=1)
                  for qb in sorted(set(blocks) | {qb - 1 for qb in prev_blocks})}
            for qb in blocks:
                acc = jnp.dot(p_cur[qb], v1[qb], preferred_element_type=F32)
                if qb in p_prev:
                    acc = acc + jnp.dot(p_prev[qb], v1[qb - 1], preferred_element_type=F32)
                blk, res = qb % nb, qb // nb
                dst = pl.ds(blk * ATT_BLK * dil + res, ATT_BLK, stride=dil) if dil > 1 else rows(qb)
                o_s[g, dst, :] = jnp.where(first_head, acc[:ATT_BLK, :LANES], acc[ATT_BLK:, :LANES])
                l_s[g, dst, :] = jnp.where(first_head, acc[:ATT_BLK, LANES:], acc[ATT_BLK:, LANES:])
                m_s[g, dst, :] = jnp.where(first_head, m[qb, 0], m[qb, 1])

    m_all = m_s[0]
    for g in range(1, ng):
        m_all = jnp.maximum(m_all, m_s[g])
    num = jnp.zeros((seq, LANES), F32)
    den = jnp.zeros((seq, LANES), F32)
    for g in range(ng):
        e = jnp.exp(m_s[g] - m_all)
        num = num + e * o_s[g]
        den = den + e * l_s[g]
    out_ref[...] = (num / den).astype(out_ref.dtype)


def attn_prompt(qkv_groups, batch, seq):
    ng = len(ATT_GROUPS)
    hpairs = ATT_HEADS * ATT_DH // LANES
    in_specs, args = [], []
    for g in range(ng):
        for part in range(3):
            in_specs.append(pl.BlockSpec((seq, LANES), lambda b, h, part=part: (b, part * hpairs + h)))
            args.append(qkv_groups[g])
    return pl.pallas_call(
        functools.partial(_attn_prompt_kernel, seq),
        grid=(batch, hpairs),
        in_specs=in_specs,
        out_specs=pl.BlockSpec((seq, LANES), lambda b, h: (b, h)),
        out_shape=jax.ShapeDtypeStruct((batch * seq, D_MODEL), BF16),
        scratch_shapes=[pltpu.VMEM((ng, seq, LANES), F32)] * 3,
        compiler_params=_cparams("parallel", "parallel"),
        name="attn_prompt",
    )(*args)


def _attn_sample_kernel(hb, qkv_ref, slope_ref, c0_ref, c1_ref, c2_ref, o_ref):
    hg = pl.program_id(0)
    b = pl.program_id(1)
    caches = (c0_ref, c1_ref, c2_ref)
    ng = len(ATT_GROUPS)
    scale = ATT_DH ** -0.5

    @pl.when(b == 0)
    def _():
        o_ref[...] = jnp.zeros_like(o_ref)

    this_seq = lax.broadcasted_iota(jnp.int32, (ATT_DH, LANES), 1) == b
    pick = jnp.where(lax.broadcasted_iota(jnp.int32, (LANES, LANES), 0) == b, 1.0, 0.0).astype(BF16)
    lane_b = [[jnp.dot(qkv_ref[0, g, part], pick, preferred_element_type=F32)
               + jnp.dot(qkv_ref[1, g, part], pick, preferred_element_type=F32)
               for part in range(3)] for g in range(ng)]

    for hh in range(hb):
        rows = slice(hh * ATT_DH, (hh + 1) * ATT_DH)
        cols = [[lane_b[g][part][rows] for part in range(3)] for g in range(ng)]
        scores, new_scores = [], []
        m = None
        for g, (_, dil) in enumerate(ATT_GROUPS):
            wb = caches[g].shape[-1]
            qc = cols[g][0] * scale
            s = jnp.sum(caches[g][0, hh] * jnp.concatenate([qc] * (wb // LANES), axis=1),
                        axis=0, keepdims=True)
            pos = lax.broadcasted_iota(jnp.int32, (1, wb), 1)
            slope = slope_ref[g, hg * hb + hh]
            s = jnp.where((pos & (dil - 1)) == 0, s - slope * (wb - pos).astype(F32), NEG_INF)
            s_new = jnp.sum(qc * cols[g][1], axis=0, keepdims=True)[:, :1]
            mg = jnp.maximum(jnp.max(s, axis=1, keepdims=True), s_new)
            m = mg if m is None else jnp.maximum(m, mg)
            scores.append(s)
            new_scores.append(s_new)
        l = jnp.zeros((1, 1), F32)
        o = jnp.zeros((ATT_DH, 1), F32)
        for g in range(ng):
            p = jnp.exp(scores[g] - m)
            p_new = jnp.exp(new_scores[g] - m)
            l = l + jnp.sum(p, axis=1, keepdims=True) + p_new
            o = o + jnp.sum(caches[g][1, hh] * p, axis=1, keepdims=True) + p_new * cols[g][2]
        o_ref[rows, :] = jnp.where(this_seq, o / l, o_ref[rows, :])


def attn_sample(proj, caches, layer, hb=8):
    n = proj.shape[0]
    ng = len(ATT_GROUPS)
    hd = ATT_HEADS * ATT_DH
    assert n == LANES
    for c, (win, dil) in zip(caches, ATT_GROUPS):
        assert c.shape[2] == win and win % dil == 0 and dil & (dil - 1) == 0
    nall = ng * ATT_HEADS
    slopes = jnp.exp2(-8.0 * jnp.arange(1, nall + 1, dtype=F32) / nall).reshape(ng, ATT_HEADS)
    proj_hi = proj.astype(BF16)
    proj_lo = (proj - proj_hi.astype(F32)).astype(BF16)
    qkv_t = jnp.stack([proj_hi.T, proj_lo.T]).reshape(2, ng, 3, hd, n)
    cache_t = [jnp.transpose(c, (0, 1, 3, 4, 5, 2)) for c in caches]
    cache_specs = [pl.BlockSpec((None, None, 2, hb, ATT_DH, c.shape[-1]), lambda h, b: (layer, b, 0, h, 0, 0))
                   for c in cache_t]
    out_t = pl.pallas_call(
        functools.partial(_attn_sample_kernel, hb),
        grid=(ATT_HEADS // hb, n),
        in_specs=[pl.BlockSpec((2, ng, 3, hb * ATT_DH, n), lambda h, b: (0, 0, 0, h, 0)),
                  pl.BlockSpec(memory_space=pltpu.SMEM)] + cache_specs,
        out_specs=pl.BlockSpec((hb * ATT_DH, n), lambda h, b: (h, 0)),
        out_shape=jax.ShapeDtypeStruct((hd, n), F32),
        compiler_params=_cparams("parallel", "arbitrary"),
        name="attn_sample",
    )(qkv_t, slopes, *cache_t)
    return out_t.T


def kernel(x_prompt, x_sample, state_ret, state_pool, cache_kv_g0, cache_kv_g1, cache_kv_g2, norm_gains,
           w_in_ret, w_out_ret, w_grp_pool, scale_pool, w_in_attn, w_out_attn, w_ffn_in, w_ffn_out):
    batch, seq, d = x_prompt.shape
    ns = x_sample.shape[0]
    caches = (cache_kv_g0, cache_kv_g1, cache_kv_g2)
    ng = len(ATT_GROUPS)
    xp = x_prompt.reshape(batch * seq, d)
    xs = x_sample.reshape(ns, d)
    ret_p, pool_p, pool_s = [], [], []
    ret_s = None
    kv_p = [[] for _ in range(ng)]
    kv_s = [[] for _ in range(ng)]

    w_in_ret, w_out_ret, w_grp_pool, w_in_attn, w_out_attn, w_ffn_in, w_ffn_out = (
        w.astype(BF16) for w in (w_in_ret, w_out_ret, w_grp_pool, w_in_attn, w_out_attn, w_ffn_in, w_ffn_out))

    for i in range(DEPTH):
        kind, j = i % N_MIXERS, i // N_MIXERS
        g = [norm_gains[i, r][None, :] for r in range(4)]
        if kind == 0:
            proj = norm_matmul(xp, g[0], w_in_ret, j, BF16, 512)
            o, sp = ret_prompt(proj, batch, seq)
            xp = matmul_norm_res(o, w_out_ret, j, g[1], xp, 512)
            proj_s = norm_matmul(xs, g[0], w_in_ret, j, F32, ns)
            o_s, ret_s = ret_sample(proj_s, state_ret, j, ret_s)
            xs = matmul_norm_res(o_s, w_out_ret, j, g[1], xs, ns)
            ret_p.append(sp)
        elif kind == 1:
            scale = scale_pool[j][None, :]
            xp3, last = pool_prompt(xp.reshape(batch, seq, d), g[0], w_grp_pool, j, scale, g[1])
            xp = xp3.reshape(batch * seq, d)
            xs, nst = pool_sample(xs, state_pool[j].reshape(ns, POOL_PREV * d), g[0], w_grp_pool, j, scale, g[1])
            pool_p.append(last[:, POOL_HALO - POOL_PREV:])
            pool_s.append(nst.reshape(ns, POOL_PREV, d))
        else:
            x3 = xp.reshape(batch, seq, d)
            qkv = []
            for gi in range(ng):
                perm, kv = attn_proj(x3, g[0], w_in_attn, j, gi)
                qkv.append(perm.reshape(batch * seq, ATT_GW))
                kv_p[gi].append(kv.reshape(batch, kv.shape[1], 2, ATT_HEADS, ATT_DH))
            a = attn_prompt(qkv, batch, seq)
            xp = matmul_norm_res(a, w_out_attn, j, g[1], xp, 512)
            proj_s = norm_matmul(xs, g[0], w_in_attn, j, F32, ns)
            a_s = attn_sample(proj_s, caches, j)
            xs = matmul_norm_res(a_s.astype(BF16), w_out_attn, j, g[1], xs, ns)
            p5 = proj_s.reshape(ns, 1, ng, 3, ATT_HEADS, ATT_DH)
            for gi in range(ng):
                kv_s[gi].append(p5[:, :, gi, 1:])
        xp = ffn(xp, g[2], w_ffn_in, w_ffn_out, i, g[3], 512)
        xs = ffn(xs, g[2], w_ffn_in, w_ffn_out, i, g[3], ns)

    return (xp.reshape(batch, seq, d), xs.reshape(ns, 1, d),
            jnp.stack(ret_p), ret_s,
            jnp.stack(pool_p), jnp.stack(pool_s),
            jnp.stack(kv_p[0]), jnp.stack(kv_s[0]),
            jnp.stack(kv_p[1]), jnp.stack(kv_s[1]),
            jnp.stack(kv_p[2]), jnp.stack(kv_s[2]))
```

```python
import functools

import jax
import jax.numpy as jnp
from jax import lax
from jax.experimental import pallas as pl
from jax.experimental.pallas import tpu as pltpu

F32 = jnp.float32
BF16 = jnp.bfloat16

D_MODEL = 1024
DEPTH = 4
PAST_LEN = 2048
N_MIXERS = 3

RET_HEADS = 4
RET_DK = D_MODEL // RET_HEADS
RET_DV = 2 * D_MODEL // RET_HEADS
RET_CHUNK = 128

POOL_WINDOWS = (2, 4, 8, 16)
POOL_GW = D_MODEL // len(POOL_WINDOWS)
POOL_PREV = max(POOL_WINDOWS) - 1
POOL_HALO = POOL_PREV + 1

ATT_GROUPS = ((128, 1), (512, 4), (2048, 16))
ATT_HEADS = 16
ATT_DH = D_MODEL // ATT_HEADS
ATT_GW = 3 * ATT_HEADS * ATT_DH
ATT_BLK = 128
ATT_INTERLEAVE = 4

FFN_HIDDEN = 2816

NORM_EPS = 1e-6
GN_EPS = 1e-5
NEG_INF = -1e30

LANES = 128
VMEM_LIMIT = 56 * 1024 * 1024


def _cparams(*sem):
    return pltpu.CompilerParams(dimension_semantics=sem, vmem_limit_bytes=VMEM_LIMIT)


def _rms(x, g):
    return x * lax.rsqrt(jnp.mean(x * x, axis=-1, keepdims=True) + NORM_EPS) * g


def _silu(x):
    return x * jax.nn.sigmoid(x)


def _resident(shape):
    return pl.BlockSpec(shape, lambda *_: (0,) * len(shape), pipeline_mode=pl.Buffered(1))


def _layer_resident(w, layer):
    return pl.BlockSpec((None,) + w.shape[1:], lambda *_: (layer,) + (0,) * (w.ndim - 1),
                        pipeline_mode=pl.Buffered(1))


def _norm_matmul_kernel(tn, x_ref, g_ref, w_ref, o_ref):
    xn = _rms(x_ref[...], g_ref[...]).astype(BF16)
    for j in range(w_ref.shape[1] // tn):
        cols = slice(j * tn, (j + 1) * tn)
        o_ref[:, cols] = jnp.dot(xn, w_ref[:, cols], preferred_element_type=F32).astype(o_ref.dtype)


def norm_matmul(x, g, w, layer, out_dtype, tm, tn=512):
    m, d = x.shape
    n = w.shape[2]
    return pl.pallas_call(
        functools.partial(_norm_matmul_kernel, tn),
        grid=(m // tm,),
        in_specs=[pl.BlockSpec((tm, d), lambda i: (i, 0)), _resident((1, d)), _layer_resident(w, layer)],
        out_specs=pl.BlockSpec((tm, n), lambda i: (i, 0)),
        out_shape=jax.ShapeDtypeStruct((m, n), out_dtype),
        compiler_params=_cparams("parallel"),
        name="norm_matmul",
    )(x, g, w)


def _matmul_norm_res_kernel(a_ref, w_ref, g_ref, x_ref, o_ref):
    y = jnp.dot(a_ref[...], w_ref[...], preferred_element_type=F32)
    o_ref[...] = x_ref[...] + _rms(y, g_ref[...])


def matmul_norm_res(a, w, layer, g, x, tm):
    m, k = a.shape
    d = w.shape[2]
    return pl.pallas_call(
        _matmul_norm_res_kernel,
        grid=(m // tm,),
        in_specs=[pl.BlockSpec((tm, k), lambda i: (i, 0)),
                  _layer_resident(w, layer),
                  _resident((1, d)),
                  pl.BlockSpec((tm, d), lambda i: (i, 0))],
        out_specs=pl.BlockSpec((tm, d), lambda i: (i, 0)),
        out_shape=jax.ShapeDtypeStruct((m, d), F32),
        compiler_params=_cparams("parallel"),
        name="matmul_norm_res",
    )(a, w, g, x)


def _ffn_kernel(th, x_ref, g2_ref, wi_ref, wo_ref, g3_ref, o_ref, a_ref):
    x = x_ref[...]
    xn = _rms(x, g2_ref[...]).astype(BF16)
    for j in range(FFN_HIDDEN // th):
        gate = jnp.dot(xn, wi_ref[:, j * th:(j + 1) * th], preferred_element_type=F32)
        up = jnp.dot(xn, wi_ref[:, FFN_HIDDEN + j * th:FFN_HIDDEN + (j + 1) * th], preferred_element_type=F32)
        a_ref[:, j * th:(j + 1) * th] = (_silu(gate) * up).astype(BF16)
    y = jnp.dot(a_ref[...], wo_ref[...], preferred_element_type=F32)
    o_ref[...] = x + _rms(y, g3_ref[...])


def ffn(x, g2, w_in, w_out, layer, g3, tm, th=256):
    m, d = x.shape
    return pl.pallas_call(
        functools.partial(_ffn_kernel, th),
        grid=(m // tm,),
        in_specs=[pl.BlockSpec((tm, d), lambda i: (i, 0)),
                  _resident((1, d)), _layer_resident(w_in, layer), _layer_resident(w_out, layer),
                  _resident((1, d))],
        out_specs=pl.BlockSpec((tm, d), lambda i: (i, 0)),
        out_shape=jax.ShapeDtypeStruct((m, d), F32),
        scratch_shapes=[pltpu.VMEM((tm, FFN_HIDDEN), BF16)],
        compiler_params=_cparams("parallel"),
        name="ffn",
    )(x, g2, w_in, w_out, g3)


def _ret_tables(c):
    lg = jnp.log1p(-jnp.exp2(-5.0 - jnp.arange(RET_HEADS, dtype=F32)))
    n = jnp.arange(c, dtype=F32)
    diff = n[:, None] - n[None, :]
    decay = jnp.where(diff[None] >= 0, jnp.exp(jnp.maximum(diff, 0.0)[None] * lg[:, None, None]), 0.0)
    xi = jnp.exp((n[:, None] + 1.0) * lg[None, :])
    zeta = jnp.exp((c - 1.0 - n)[:, None] * lg[None, :])
    g_c = jnp.exp(c * lg)
    return decay, xi.T[:, :, None], zeta.T[:, :, None], g_c[:, None, None]


def _group_norm_gate(o, gate):
    mu = jnp.mean(o, axis=-1, keepdims=True)
    oc = o - mu
    var = jnp.mean(oc * oc, axis=-1, keepdims=True)
    return oc * lax.rsqrt(var + GN_EPS) * _silu(gate)


def _ret_prompt_kernel(nsub, p_ref, decay_ref, xi_ref, zeta_ref, gc_ref, o_ref, s_ref):
    @pl.when(pl.program_id(1) == 0)
    def _():
        s_ref[...] = jnp.zeros_like(s_ref)

    c = RET_CHUNK
    hk = RET_HEADS * RET_DK
    hv = RET_HEADS * RET_DV
    nt = (((1,), (1,)), ((), ()))
    for h in range(RET_HEADS):
        s = s_ref[h]
        for cc in range(nsub):
            rows = slice(cc * c, (cc + 1) * c)
            q = p_ref[rows, h * RET_DK:(h + 1) * RET_DK] * (RET_DK ** -0.5)
            k = p_ref[rows, hk + h * RET_DK:hk + (h + 1) * RET_DK]
            v = p_ref[rows, 2 * hk + h * RET_DV:2 * hk + (h + 1) * RET_DV]
            gt = p_ref[rows, 2 * hk + hv + h * RET_DV:2 * hk + hv + (h + 1) * RET_DV]
            a = lax.dot_general(q, k, nt, preferred_element_type=F32) * decay_ref[h]
            o = (jnp.dot(a.astype(BF16), v, preferred_element_type=F32)
                 + jnp.dot(q, s.astype(BF16), preferred_element_type=F32) * xi_ref[h])
            kz = (k.astype(F32) * zeta_ref[h]).T.astype(BF16)
            s = gc_ref[h] * s + jnp.dot(kz, v, preferred_element_type=F32)
            o_ref[rows, h * RET_DV:(h + 1) * RET_DV] = _group_norm_gate(o, gt.astype(F32)).astype(o_ref.dtype)
        s_ref[h] = s


def ret_prompt(proj, batch, seq, nsub=2):
    rows = nsub * RET_CHUNK
    nstep = seq // rows
    decay, xi, zeta, g_c = _ret_tables(RET_CHUNK)
    full = lambda a: pl.BlockSpec(a.shape, lambda b, i: (0,) * a.ndim)
    return pl.pallas_call(
        functools.partial(_ret_prompt_kernel, nsub),
        grid=(batch, nstep),
        in_specs=[pl.BlockSpec((rows, proj.shape[1]), lambda b, i: (b * nstep + i, 0)),
                  full(decay), full(xi), full(zeta), full(g_c)],
        out_specs=[pl.BlockSpec((rows, RET_HEADS * RET_DV), lambda b, i: (b * nstep + i, 0)),
                   pl.BlockSpec((None, RET_HEADS, RET_DK, RET_DV), lambda b, i: (b, 0, 0, 0))],
        out_shape=[jax.ShapeDtypeStruct((batch * seq, RET_HEADS * RET_DV), BF16),
                   jax.ShapeDtypeStruct((batch, RET_HEADS, RET_DK, RET_DV), F32)],
        compiler_params=_cparams("parallel", "arbitrary"),
        name="ret_prompt",
    )(proj, decay, xi, zeta, g_c)


def _ret_sample_kernel(bb, layer, first, qk_ref, vg_ref, qkt_ref, s0_ref, xi_ref, gc_ref, *rest):
    o_ref, s_ref = rest[-2:]
    hk = RET_HEADS * RET_DK
    hv = RET_HEADS * RET_DV
    n = qkt_ref.shape[2]
    scale = RET_DK ** -0.5
    seq_id = lax.broadcasted_iota(jnp.int32, (n, LANES), 0)
    widen = lambda col: jnp.concatenate([col] * (RET_DV // LANES), axis=1)
    for i in range(bb):
        pick = jnp.where(seq_id == pl.program_id(0) * bb + i, 1.0, 0.0).astype(BF16)
        qk_col = (jnp.dot(qkt_ref[0], pick, preferred_element_type=F32)
                  + jnp.dot(qkt_ref[1], pick, preferred_element_type=F32))
        for h in range(RET_HEADS):
            q_row = qk_ref[i:i + 1, h * RET_DK:(h + 1) * RET_DK] * scale
            k_row = qk_ref[i:i + 1, hk + h * RET_DK:hk + (h + 1) * RET_DK]
            v_row = vg_ref[i:i + 1, h * RET_DV:(h + 1) * RET_DV]
            g_row = vg_ref[i:i + 1, hv + h * RET_DV:hv + (h + 1) * RET_DV]
            q_col = widen(qk_col[h * RET_DK:(h + 1) * RET_DK] * scale)
            k_col = widen(qk_col[hk + h * RET_DK:hk + (h + 1) * RET_DK])
            s0 = s0_ref[i, h]
            a = jnp.sum(q_row * k_row, axis=-1, keepdims=True)
            qs = jnp.sum(q_col * s0, axis=0, keepdims=True)
            o = a * v_row + qs * xi_ref[h]
            s_new = gc_ref[h] * s0 + k_col * v_row
            if first:
                s_ref[layer, i, h] = s_new
            else:
                s_ref[i, h] = s_new
            o_ref[i:i + 1, h * RET_DV:(h + 1) * RET_DV] = _group_norm_gate(o, g_row).astype(o_ref.dtype)
    if first:
        for other in range(s_ref.shape[0]):
            if other != layer:
                s_ref[other] = jnp.zeros(s_ref.shape[1:], F32)


def ret_sample(proj, state, layer, new_state=None, bb=2):
    n = proj.shape[0]
    hk = RET_HEADS * RET_DK
    hv = RET_HEADS * RET_DV
    first = new_state is None
    _, xi, _, g_c = _ret_tables(1)
    qk = proj[:, :2 * hk]
    vg = proj[:, 2 * hk:]
    layer_block = (None, bb, RET_HEADS, RET_DK, RET_DV)
    layer_spec = pl.BlockSpec(layer_block, lambda i: (layer, i, 0, 0, 0))
    stack_spec = pl.BlockSpec((state.shape[0],) + layer_block[1:], lambda i: (0, i, 0, 0, 0))
    rows = lambda width: pl.BlockSpec((None, bb, width), lambda i: (i, 0, 0))
    qk_hi = qk.astype(BF16)
    qk_lo = (qk - qk_hi.astype(F32)).astype(BF16)
    qkt = jnp.stack([qk_hi.T, qk_lo.T])
    assert n == LANES
    in_specs = [rows(2 * hk), rows(2 * hv), _resident(qkt.shape), layer_spec,
                _resident((RET_HEADS, 1, 1)), _resident((RET_HEADS, 1, 1))]
    args = [qk.reshape(n // bb, bb, 2 * hk), vg.reshape(n // bb, bb, 2 * hv), qkt, state, xi, g_c]
    aliases = {}
    if not first:
        in_specs.append(pl.BlockSpec(memory_space=pl.ANY))
        args.append(new_state)
        aliases = {len(args) - 1: 1}
    o, s = pl.pallas_call(
        functools.partial(_ret_sample_kernel, bb, layer, first),
        grid=(n // bb,),
        in_specs=in_specs,
        out_specs=[rows(hv), stack_spec if first else layer_spec],
        out_shape=[jax.ShapeDtypeStruct((n // bb, bb, hv), BF16),
                   jax.ShapeDtypeStruct(state.shape, F32)],
        input_output_aliases=aliases,
        compiler_params=_cparams("parallel"),
        name="ret_sample",
    )(*args)
    return o.reshape(n, hv), s


def _pool_prompt_kernel(tl, x_ref, halo_ref, g0_ref, w_ref, sc_ref, g1_ref, o_ref, last_ref, buf_ref):
    i = pl.program_id(1)
    x = x_ref[...]
    xn = _rms(x, g0_ref[...])
    hn = _rms(halo_ref[...], g0_ref[...])
    buf_ref[0:POOL_HALO, :] = jnp.where(i > 0, hn, 0.0)
    buf_ref[POOL_HALO:POOL_HALO + tl, :] = xn
    pos = i * tl + lax.broadcasted_iota(jnp.int32, (tl, 1), 0)
    ys = []
    for gi, w in enumerate(POOL_WINDOWS):
        cols = slice(gi * POOL_GW, (gi + 1) * POOL_GW)
        cur = xn[:, cols]
        acc = cur
        for j in range(1, w):
            acc = acc + buf_ref[POOL_HALO - j:POOL_HALO - j + tl, cols]
        cnt = jnp.minimum(pos + 1, w).astype(F32)
        z = acc / cnt - cur
        ys.append(jnp.dot(z.astype(BF16), w_ref[gi], preferred_element_type=F32))
    y = jnp.concatenate(ys, axis=1) * sc_ref[...]
    o_ref[...] = x + _rms(y, g1_ref[...])
    last_ref[...] = buf_ref[tl:tl + POOL_HALO, :]


def pool_prompt(x, g0, w_grp, layer, scale, g1, tl=512):
    batch, seq, d = x.shape
    hb = tl // POOL_HALO
    vec = pl.BlockSpec((1, d), lambda b, i: (0, 0))
    return pl.pallas_call(
        functools.partial(_pool_prompt_kernel, tl),
        grid=(batch, seq // tl),
        in_specs=[pl.BlockSpec((None, tl, d), lambda b, i: (b, i, 0)),
                  pl.BlockSpec((None, POOL_HALO, d), lambda b, i: (b, jnp.maximum(i * hb - 1, 0), 0)),
                  vec,
                  _layer_resident(w_grp, layer),
                  vec, vec],
        out_specs=[pl.BlockSpec((None, tl, d), lambda b, i: (b, i, 0)),
                   pl.BlockSpec((None, POOL_HALO, d), lambda b, i: (b, 0, 0))],
        out_shape=[jax.ShapeDtypeStruct(x.shape, F32),
                   jax.ShapeDtypeStruct((batch, POOL_HALO, d), F32)],
        scratch_shapes=[pltpu.VMEM((POOL_HALO + tl, d), F32)],
        compiler_params=_cparams("parallel", "arbitrary"),
        name="pool_prompt",
    )(x, x, g0, w_grp, scale, g1)


def _pool_sample_kernel(x_ref, st_ref, g0_ref, w_ref, sc_ref, g1_ref, o_ref, nst_ref):
    d = D_MODEL
    x = x_ref[...]
    xn = _rms(x, g0_ref[...])
    ys = []
    for gi, w in enumerate(POOL_WINDOWS):
        acc = xn[:, gi * POOL_GW:(gi + 1) * POOL_GW]
        cur = acc
        for j in range(1, w):
            base = (POOL_PREV - j) * d + gi * POOL_GW
            acc = acc + st_ref[:, base:base + POOL_GW]
        cnt = float(min(PAST_LEN + 1, w))
        z = acc / cnt - cur
        ys.append(jnp.dot(z.astype(BF16), w_ref[gi], preferred_element_type=F32))
    y = jnp.concatenate(ys, axis=1) * sc_ref[...]
    o_ref[...] = x + _rms(y, g1_ref[...])
    nst_ref[:, :(POOL_PREV - 1) * d] = st_ref[:, d:]
    nst_ref[:, (POOL_PREV - 1) * d:] = xn


def pool_sample(x, state, g0, w_grp, layer, scale, g1, tb=32):
    n, d = x.shape
    vec = pl.BlockSpec((1, d), lambda i: (0, 0))
    return pl.pallas_call(
        _pool_sample_kernel,
        grid=(n // tb,),
        in_specs=[pl.BlockSpec((tb, d), lambda i: (i, 0)),
                  pl.BlockSpec((tb, POOL_PREV * d), lambda i: (i, 0)),
                  vec,
                  _layer_resident(w_grp, layer),
                  vec, vec],
        out_specs=[pl.BlockSpec((tb, d), lambda i: (i, 0)),
                   pl.BlockSpec((tb, POOL_PREV * d), lambda i: (i, 0))],
        out_shape=[jax.ShapeDtypeStruct((n, d), F32),
                   jax.ShapeDtypeStruct(state.shape, F32)],
        compiler_params=_cparams("parallel"),
        name="pool_sample",
    )(x, state, g0, w_grp, scale, g1)


def _attn_proj_kernel(dil, tn, x_ref, g_ref, w_ref, perm_ref, kv_ref, res_ref):
    rows, kv_cols = kv_ref.shape
    t = x_ref.shape[0]
    nq = w_ref.shape[1] - kv_cols
    xn = _rms(x_ref[...], g_ref[...]).astype(BF16)
    for j in range(w_ref.shape[1] // tn):
        cols = slice(j * tn, (j + 1) * tn)
        res = jnp.dot(xn, w_ref[:, cols], preferred_element_type=F32)
        if dil == 1:
            perm_ref[0, :, cols] = res.astype(BF16)
        else:
            for c in range(tn // LANES):
                slot = (j * (tn // LANES) + c) % res_ref.shape[0]
                res_ref[slot] = res[:, c * LANES:(c + 1) * LANES]
                for r in range(dil):
                    perm_ref[r, :, j * tn + c * LANES:j * tn + (c + 1) * LANES] = (
                        res_ref[slot, pl.ds(r, t // dil, stride=dil), :].astype(BF16))
        if j * tn >= nq:
            kv_ref[:, j * tn - nq:(j + 1) * tn - nq] = res[t - rows:, :]


def attn_proj(x, g, w, layer, gi, t=1024, tn=256):
    batch, seq, d = x.shape
    win, dil = ATT_GROUPS[gi]
    kv_rows = min(win, seq)
    nq = ATT_HEADS * ATT_DH
    tkv = min(t, kv_rows)
    first = (seq - kv_rows) // t
    kv_index = lambda b, i: (b, jnp.maximum(i - first, 0), 0)
    perm, kv = pl.pallas_call(
        functools.partial(_attn_proj_kernel, dil, tn),
        grid=(batch, seq // t),
        in_specs=[pl.BlockSpec((None, t, d), lambda b, i: (b, i, 0)),
                  _resident((1, d)),
                  pl.BlockSpec((None, d, ATT_GW), lambda b, i: (layer, 0, gi), pipeline_mode=pl.Buffered(1))],
        out_specs=[pl.BlockSpec((None, dil, t // dil, ATT_GW), lambda b, i: (b, 0, i, 0)),
                   pl.BlockSpec((None, tkv, ATT_GW - nq), kv_index)],
        out_shape=[jax.ShapeDtypeStruct((batch, dil, seq // dil, ATT_GW), BF16),
                   jax.ShapeDtypeStruct((batch, kv_rows, ATT_GW - nq), F32)],
        scratch_shapes=[pltpu.VMEM((2 * tn // LANES, t, LANES), F32)],
        compiler_params=_cparams("parallel", "arbitrary"),
        name="attn_proj_g%d" % gi,
    )(x, g, w)
    return perm.reshape(batch, seq, ATT_GW), kv


def _alibi_slope(group, head):
    n = len(ATT_GROUPS) * ATT_HEADS
    idx = jnp.full((ATT_BLK, ATT_BLK), group * ATT_HEADS + 1, F32) + head
    return jnp.exp2(-8.0 * idx / n)


def _attn_prompt_kernel(seq, *refs):
    ng = len(ATT_GROUPS)
    qkv = [refs[3 * g:3 * g + 3] for g in range(ng)]
    out_ref, o_s, m_s, l_s = refs[3 * ng:]
    hp = pl.program_id(1)
    lane = lax.broadcasted_iota(jnp.int32, (ATT_BLK, LANES), 1)
    first_head = lane < ATT_DH
    qi = lax.broadcasted_iota(jnp.int32, (ATT_BLK, ATT_BLK), 0)
    kj = lax.broadcasted_iota(jnp.int32, (ATT_BLK, ATT_BLK), 1)
    scale = ATT_DH ** -0.5
    nt = (((1,), (1,)), ((), ()))
    ones = jnp.ones((2 * ATT_BLK, LANES), BF16)

    for g, (win, dil) in enumerate(ATT_GROUPS):
        q_ref, k_ref, v_ref = qkv[g]
        n_keys = win // dil
        ls = seq // dil
        nb = ls // ATT_BLK
        d_cur = qi - kj
        d_prev = qi + ATT_BLK - kj
        bias_cur, bias_prev = [], []
        for hh in range(2):
            slope = _alibi_slope(g, (hp * 2 + hh).astype(F32))
            bias_cur.append(jnp.where(d_cur >= 0, -slope * (dil * d_cur).astype(F32), NEG_INF))
            bias_prev.append(jnp.where(d_prev <= n_keys, -slope * (dil * d_prev).astype(F32), NEG_INF))
        bias_cur = jnp.concatenate(bias_cur, axis=0)
        bias_prev = jnp.concatenate(bias_prev, axis=0)
        rows = lambda qb: slice(qb * ATT_BLK, (qb + 1) * ATT_BLK)

        bias_both = jnp.concatenate([bias_prev, bias_cur], axis=1)

        def keys(qb):
            return slice((qb - 1) * ATT_BLK, (qb + 1) * ATT_BLK) if qb % nb > 0 else rows(qb)

        for c0 in range(0, seq // ATT_BLK, ATT_INTERLEAVE):
            blocks = range(c0, c0 + ATT_INTERLEAVE)
            qm = {}
            for qb in blocks:
                qf = q_ref[rows(qb), :].astype(F32) * scale
                qm[qb] = jnp.concatenate([jnp.where(first_head, qf, 0.0), jnp.where(first_head, 0.0, qf)],
                                         axis=0).astype(BF16)
            s = {qb: lax.dot_general(qm[qb], k_ref[keys(qb), :], nt, preferred_element_type=F32)
                 + (bias_both if qb % nb > 0 else bias_cur) for qb in blocks}
            halves = (slice(0, ATT_BLK), slice(ATT_BLK, 2 * ATT_BLK))
            m = {(qb, hh): jnp.max(s[qb][half], axis=-1, keepdims=True)
                 for qb in blocks for hh, half in enumerate(halves)}
            p = {qb: jnp.concatenate([jnp.exp(s[qb][half] - m[qb, hh]) for hh, half in enumerate(halves)],
                                     axis=0).astype(BF16) for qb in blocks}
            for qb in blocks:
                v = v_ref[keys(qb), :]
                acc = jnp.dot(p[qb], jnp.concatenate([v, ones[:v.shape[0]]], axis=1), preferred_element_type=F32)
                blk, res = qb % nb, qb // nb
                dst = pl.ds(blk * ATT_BLK * dil + res, ATT_BLK, stride=dil) if dil > 1 else rows(qb)
                o_s[g, dst, :] = jnp.where(first_head, acc[:ATT_BLK, :LANES], acc[ATT_BLK:, :LANES])
                l_s[g, dst, :] = jnp.where(first_head, acc[:ATT_BLK, LANES:], acc[ATT_BLK:, LANES:])
                m_s[g, dst, :] = jnp.where(first_head, m[qb, 0], m[qb, 1])

    m_all = m_s[0]
    for g in range(1, ng):
        m_all = jnp.maximum(m_all, m_s[g])
    num = jnp.zeros((seq, LANES), F32)
    den = jnp.zeros((seq, LANES), F32)
    for g in range(ng):
        e = jnp.exp(m_s[g] - m_all)
        num = num + e * o_s[g]
        den = den + e * l_s[g]
    out_ref[...] = (num / den).astype(out_ref.dtype)


def attn_prompt(qkv_groups, batch, seq):
    ng = len(ATT_GROUPS)
    hpairs = ATT_HEADS * ATT_DH // LANES
    in_specs, args = [], []
    for g in range(ng):
        for part in range(3):
            in_specs.append(pl.BlockSpec((seq, LANES), lambda b, h, part=part: (b, part * hpairs + h)))
            args.append(qkv_groups[g])
    return pl.pallas_call(
        functools.partial(_attn_prompt_kernel, seq),
        grid=(batch, hpairs),
        in_specs=in_specs,
        out_specs=pl.BlockSpec((seq, LANES), lambda b, h: (b, h)),
        out_shape=jax.ShapeDtypeStruct((batch * seq, D_MODEL), BF16),
        scratch_shapes=[pltpu.VMEM((ng, seq, LANES), F32)] * 3,
        compiler_params=_cparams("parallel", "parallel"),
        name="attn_prompt",
    )(*args)


def _attn_sample_kernel(hb, qkv_ref, slope_ref, c0_ref, c1_ref, c2_ref, o_ref):
    hg = pl.program_id(0)
    b = pl.program_id(1)
    caches = (c0_ref, c1_ref, c2_ref)
    ng = len(ATT_GROUPS)
    scale = ATT_DH ** -0.5

    @pl.when(b == 0)
    def _():
        o_ref[...] = jnp.zeros_like(o_ref)

    this_seq = lax.broadcasted_iota(jnp.int32, (ATT_DH, LANES), 1) == b
    pick = jnp.where(lax.broadcasted_iota(jnp.int32, (LANES, LANES), 0) == b, 1.0, 0.0).astype(BF16)
    lane_b = [[jnp.dot(qkv_ref[0, g, part], pick, preferred_element_type=F32)
               + jnp.dot(qkv_ref[1, g, part], pick, preferred_element_type=F32)
               for part in range(3)] for g in range(ng)]

    for hh in range(hb):
        rows = slice(hh * ATT_DH, (hh + 1) * ATT_DH)
        cols = [[lane_b[g][part][rows] for part in range(3)] for g in range(ng)]
        scores, new_scores = [], []
        m = None
        for g, (_, dil) in enumerate(ATT_GROUPS):
            wb = caches[g].shape[-1]
            qc = cols[g][0] * scale
            s = jnp.sum(caches[g][0, hh] * jnp.concatenate([qc] * (wb // LANES), axis=1),
                        axis=0, keepdims=True)
            pos = lax.broadcasted_iota(jnp.int32, (1, wb), 1)
            slope = slope_ref[g, hg * hb + hh]
            s = jnp.where((pos & (dil - 1)) == 0, s - slope * (wb - pos).astype(F32), NEG_INF)
            s_new = jnp.sum(qc * cols[g][1], axis=0, keepdims=True)[:, :1]
            mg = jnp.maximum(jnp.max(s, axis=1, keepdims=True), s_new)
            m = mg if m is None else jnp.maximum(m, mg)
            scores.append(s)
            new_scores.append(s_new)
        l = jnp.zeros((1, 1), F32)
        o = jnp.zeros((ATT_DH, 1), F32)
        for g in range(ng):
            p = jnp.exp(scores[g] - m)
            p_new = jnp.exp(new_scores[g] - m)
            l = l + jnp.sum(p, axis=1, keepdims=True) + p_new
            o = o + jnp.sum(caches[g][1, hh] * p, axis=1, keepdims=True) + p_new * cols[g][2]
        o_ref[rows, :] = jnp.where(this_seq, o / l, o_ref[rows, :])


def attn_sample(proj, caches, layer, hb=8):
    n = proj.shape[0]
    ng = len(ATT_GROUPS)
    hd = ATT_HEADS * ATT_DH
    assert n == LANES
    for c, (win, dil) in zip(caches, ATT_GROUPS):
        assert c.shape[2] == win and win % dil == 0 and dil & (dil - 1) == 0
    nall = ng * ATT_HEADS
    slopes = jnp.exp2(-8.0 * jnp.arange(1, nall + 1, dtype=F32) / nall).reshape(ng, ATT_HEADS)
    proj_hi = proj.astype(BF16)
    proj_lo = (proj - proj_hi.astype(F32)).astype(BF16)
    qkv_t = jnp.stack([proj_hi.T, proj_lo.T]).reshape(2, ng, 3, hd, n)
    cache_t = [jnp.transpose(c, (0, 1, 3, 4, 5, 2)) for c in caches]
    cache_specs = [pl.BlockSpec((None, None, 2, hb, ATT_DH, c.shape[-1]), lambda h, b: (layer, b, 0, h, 0, 0))
                   for c in cache_t]
    out_t = pl.pallas_call(
        functools.partial(_attn_sample_kernel, hb),
        grid=(ATT_HEADS // hb, n),
        in_specs=[pl.BlockSpec((2, ng, 3, hb * ATT_DH, n), lambda h, b: (0, 0, 0, h, 0)),
                  pl.BlockSpec(memory_space=pltpu.SMEM)] + cache_specs,
        out_specs=pl.BlockSpec((hb * ATT_DH, n), lambda h, b: (h, 0)),
        out_shape=jax.ShapeDtypeStruct((hd, n), F32),
        compiler_params=_cparams("parallel", "arbitrary"),
        name="attn_sample",
    )(qkv_t, slopes, *cache_t)
    return out_t.T


def kernel(x_prompt, x_sample, state_ret, state_pool, cache_kv_g0, cache_kv_g1, cache_kv_g2, norm_gains,
           w_in_ret, w_out_ret, w_grp_pool, scale_pool, w_in_attn, w_out_attn, w_ffn_in, w_ffn_out):
    batch, seq, d = x_prompt.shape
    ns = x_sample.shape[0]
    caches = (cache_kv_g0, cache_kv_g1, cache_kv_g2)
    ng = len(ATT_GROUPS)
    xp = x_prompt.reshape(batch * seq, d)
    xs = x_sample.reshape(ns, d)
    ret_p, pool_p, pool_s = [], [], []
    ret_s = None
    kv_p = [[] for _ in range(ng)]
    kv_s = [[] for _ in range(ng)]

    w_in_ret, w_out_ret, w_grp_pool, w_in_attn, w_out_attn, w_ffn_in, w_ffn_out = (
        w.astype(BF16) for w in (w_in_ret, w_out_ret, w_grp_pool, w_in_attn, w_out_attn, w_ffn_in, w_ffn_out))

    for i in range(DEPTH):
        kind, j = i % N_MIXERS, i // N_MIXERS
        g = [norm_gains[i, r][None, :] for r in range(4)]
        if kind == 0:
            proj = norm_matmul(xp, g[0], w_in_ret, j, BF16, 512)
            o, sp = ret_prompt(proj, batch, seq)
            xp = matmul_norm_res(o, w_out_ret, j, g[1], xp, 512)
            proj_s = norm_matmul(xs, g[0], w_in_ret, j, F32, ns)
            o_s, ret_s = ret_sample(proj_s, state_ret, j, ret_s)
            xs = matmul_norm_res(o_s, w_out_ret, j, g[1], xs, ns)
            ret_p.append(sp)
        elif kind == 1:
            scale = scale_pool[j][None, :]
            xp3, last = pool_prompt(xp.reshape(batch, seq, d), g[0], w_grp_pool, j, scale, g[1])
            xp = xp3.reshape(batch * seq, d)
            xs, nst = pool_sample(xs, state_pool[j].reshape(ns, POOL_PREV * d), g[0], w_grp_pool, j, scale, g[1])
            pool_p.append(last[:, POOL_HALO - POOL_PREV:])
            pool_s.append(nst.reshape(ns, POOL_PREV, d))
        else:
            x3 = xp.reshape(batch, seq, d)
            qkv = []
            for gi in range(ng):
                perm, kv = attn_proj(x3, g[0], w_in_attn, j, gi)
                qkv.append(perm.reshape(batch * seq, ATT_GW))
                kv_p[gi].append(kv.reshape(batch, kv.shape[1], 2, ATT_HEADS, ATT_DH))
            a = attn_prompt(qkv, batch, seq)
            xp = matmul_norm_res(a, w_out_attn, j, g[1], xp, 512)
            proj_s = norm_matmul(xs, g[0], w_in_attn, j, F32, ns)
            a_s = attn_sample(proj_s, caches, j)
            xs = matmul_norm_res(a_s.astype(BF16), w_out_attn, j, g[1], xs, ns)
            p5 = proj_s.reshape(ns, 1, ng, 3, ATT_HEADS, ATT_DH)
            for gi in range(ng):
                kv_s[gi].append(p5[:, :, gi, 1:])
        xp = ffn(xp, g[2], w_ffn_in, w_ffn_out, i, g[3], 512)
        xs = ffn(xs, g[2], w_ffn_in, w_ffn_out, i, g[3], ns)

    return (xp.reshape(batch, seq, d), xs.reshape(ns, 1, d),
            jnp.stack(ret_p), ret_s,
            jnp.stack(pool_p), jnp.stack(pool_s),
            jnp.stack(kv_p[0]), jnp.stack(kv_s[0]),
            jnp.stack(kv_p[1]), jnp.stack(kv_s[1]),
            jnp.stack(kv_p[2]), jnp.stack(kv_s[2]))
```

```python
import functools

import jax
import jax.numpy as jnp
from jax import lax
from jax.experimental import pallas as pl
from jax.experimental.pallas import tpu as pltpu

F32 = jnp.float32
BF16 = jnp.bfloat16

D_MODEL = 1024
DEPTH = 4
PAST_LEN = 2048
N_MIXERS = 3

RET_HEADS = 4
RET_DK = D_MODEL // RET_HEADS
RET_DV = 2 * D_MODEL // RET_HEADS
RET_PROMPT_CHUNK = 256

POOL_WINDOWS = (2, 4, 8, 16)
POOL_GW = D_MODEL // len(POOL_WINDOWS)
POOL_PREV = max(POOL_WINDOWS) - 1
POOL_HALO = POOL_PREV + 1

ATT_GROUPS = ((128, 1), (512, 4), (2048, 16))
ATT_HEADS = 16
ATT_DH = D_MODEL // ATT_HEADS
ATT_GW = 3 * ATT_HEADS * ATT_DH
ATT_BLK = 128
ATT_REGROUP_RADIX = 4
ATT_INTERLEAVE = 4

FFN_HIDDEN = 2816

NORM_EPS = 1e-6
GN_EPS = 1e-5
NEG_INF = -1e30

LANES = 128
VMEM_LIMIT = 56 * 1024 * 1024


def _cparams(*sem):
    return pltpu.CompilerParams(dimension_semantics=sem, vmem_limit_bytes=VMEM_LIMIT)


def _rms(x, g):
    return x * lax.rsqrt(jnp.mean(x * x, axis=-1, keepdims=True) + NORM_EPS) * g


def _silu(x):
    return x * jax.nn.sigmoid(x)


def _resident(shape):
    return pl.BlockSpec(shape, lambda *_: (0,) * len(shape), pipeline_mode=pl.Buffered(1))


def _layer_resident(w, layer):
    return pl.BlockSpec((None,) + w.shape[1:], lambda *_: (layer,) + (0,) * (w.ndim - 1),
                        pipeline_mode=pl.Buffered(1))


def _norm_matmul_kernel(tn, x_ref, g_ref, w_ref, o_ref):
    xn = _rms(x_ref[...], g_ref[...]).astype(BF16)
    for j in range(w_ref.shape[1] // tn):
        cols = slice(j * tn, (j + 1) * tn)
        o_ref[:, cols] = jnp.dot(xn, w_ref[:, cols], preferred_element_type=F32).astype(o_ref.dtype)


def norm_matmul(x, g, w, layer, out_dtype, tm, tn=512):
    m, d = x.shape
    n = w.shape[2]
    return pl.pallas_call(
        functools.partial(_norm_matmul_kernel, tn),
        grid=(m // tm,),
        in_specs=[pl.BlockSpec((tm, d), lambda i: (i, 0)), _resident((1, d)), _layer_resident(w, layer)],
        out_specs=pl.BlockSpec((tm, n), lambda i: (i, 0)),
        out_shape=jax.ShapeDtypeStruct((m, n), out_dtype),
        compiler_params=_cparams("parallel"),
        name="norm_matmul",
    )(x, g, w)


def _matmul_norm_res_kernel(a_ref, w_ref, g_ref, x_ref, o_ref):
    y = jnp.dot(a_ref[...], w_ref[...], preferred_element_type=F32)
    o_ref[...] = x_ref[...] + _rms(y, g_ref[...])


def matmul_norm_res(a, w, layer, g, x, tm):
    m, k = a.shape
    d = w.shape[2]
    return pl.pallas_call(
        _matmul_norm_res_kernel,
        grid=(m // tm,),
        in_specs=[pl.BlockSpec((tm, k), lambda i: (i, 0)),
                  _layer_resident(w, layer),
                  _resident((1, d)),
                  pl.BlockSpec((tm, d), lambda i: (i, 0))],
        out_specs=pl.BlockSpec((tm, d), lambda i: (i, 0)),
        out_shape=jax.ShapeDtypeStruct((m, d), F32),
        compiler_params=_cparams("parallel"),
        name="matmul_norm_res",
    )(a, w, g, x)


def _ffn_kernel(th, x_ref, g2_ref, wi_ref, wo_ref, g3_ref, o_ref, a_ref):
    x = x_ref[...]
    xn = _rms(x, g2_ref[...]).astype(BF16)
    for j in range(FFN_HIDDEN // th):
        gate = jnp.dot(xn, wi_ref[:, j * th:(j + 1) * th], preferred_element_type=F32)
        up = jnp.dot(xn, wi_ref[:, FFN_HIDDEN + j * th:FFN_HIDDEN + (j + 1) * th], preferred_element_type=F32)
        a_ref[:, j * th:(j + 1) * th] = (_silu(gate) * up).astype(BF16)
    y = jnp.dot(a_ref[...], wo_ref[...], preferred_element_type=F32)
    o_ref[...] = x + _rms(y, g3_ref[...])


def ffn(x, g2, w_in, w_out, layer, g3, tm, th=256):
    m, d = x.shape
    return pl.pallas_call(
        functools.partial(_ffn_kernel, th),
        grid=(m // tm,),
        in_specs=[pl.BlockSpec((tm, d), lambda i: (i, 0)),
                  _resident((1, d)), _layer_resident(w_in, layer), _layer_resident(w_out, layer),
                  _resident((1, d))],
        out_specs=pl.BlockSpec((tm, d), lambda i: (i, 0)),
        out_shape=jax.ShapeDtypeStruct((m, d), F32),
        scratch_shapes=[pltpu.VMEM((tm, FFN_HIDDEN), BF16)],
        compiler_params=_cparams("parallel"),
        name="ffn",
    )(x, g2, w_in, w_out, g3)


def _ret_tables(c):
    lg = jnp.log1p(-jnp.exp2(-5.0 - jnp.arange(RET_HEADS, dtype=F32)))
    n = jnp.arange(c, dtype=F32)
    diff = n[:, None] - n[None, :]
    decay = jnp.where(diff[None] >= 0, jnp.exp(jnp.maximum(diff, 0.0)[None] * lg[:, None, None]), 0.0)
    xi = jnp.exp((n[:, None] + 1.0) * lg[None, :])
    zeta = jnp.exp((c - 1.0 - n)[:, None] * lg[None, :])
    g_c = jnp.exp(c * lg)
    return decay, xi.T[:, :, None], zeta.T[:, :, None], g_c[:, None, None]


def _group_norm_gate(o, gate):
    mu = jnp.mean(o, axis=-1, keepdims=True)
    oc = o - mu
    var = jnp.mean(oc * oc, axis=-1, keepdims=True)
    return oc * lax.rsqrt(var + GN_EPS) * _silu(gate)


def _ret_prompt_kernel(nsub, c, p_ref, decay_ref, xi_ref, zeta_ref, gc_ref, o_ref, s_ref):
    @pl.when(pl.program_id(1) == 0)
    def _():
        s_ref[...] = jnp.zeros_like(s_ref)

    hk = RET_HEADS * RET_DK
    hv = RET_HEADS * RET_DV
    nt = (((1,), (1,)), ((), ()))
    for h in range(RET_HEADS):
        s = s_ref[h]
        for cc in range(nsub):
            rows = slice(cc * c, (cc + 1) * c)
            q = p_ref[rows, h * RET_DK:(h + 1) * RET_DK] * (RET_DK ** -0.5)
            k = p_ref[rows, hk + h * RET_DK:hk + (h + 1) * RET_DK]
            v = p_ref[rows, 2 * hk + h * RET_DV:2 * hk + (h + 1) * RET_DV]
            gt = p_ref[rows, 2 * hk + hv + h * RET_DV:2 * hk + hv + (h + 1) * RET_DV]
            a = lax.dot_general(q, k, nt, preferred_element_type=F32) * decay_ref[h]
            o = (jnp.dot(a.astype(BF16), v, preferred_element_type=F32)
                 + jnp.dot(q, s.astype(BF16), preferred_element_type=F32) * xi_ref[h])
            kz = (k.astype(F32) * zeta_ref[h]).T.astype(BF16)
            s = gc_ref[h] * s + jnp.dot(kz, v, preferred_element_type=F32)
            o_ref[rows, h * RET_DV:(h + 1) * RET_DV] = _group_norm_gate(o, gt.astype(F32)).astype(o_ref.dtype)
        s_ref[h] = s


def ret_prompt(proj, batch, seq, chunk=RET_PROMPT_CHUNK, nsub=2):
    rows = nsub * chunk
    nstep = seq // rows
    decay, xi, zeta, g_c = _ret_tables(chunk)
    full = lambda a: pl.BlockSpec(a.shape, lambda b, i: (0,) * a.ndim)
    return pl.pallas_call(
        functools.partial(_ret_prompt_kernel, nsub, chunk),
        grid=(batch, nstep),
        in_specs=[pl.BlockSpec((rows, proj.shape[1]), lambda b, i: (b * nstep + i, 0)),
                  full(decay), full(xi), full(zeta), full(g_c)],
        out_specs=[pl.BlockSpec((rows, RET_HEADS * RET_DV), lambda b, i: (b * nstep + i, 0)),
                   pl.BlockSpec((None, RET_HEADS, RET_DK, RET_DV), lambda b, i: (b, 0, 0, 0))],
        out_shape=[jax.ShapeDtypeStruct((batch * seq, RET_HEADS * RET_DV), BF16),
                   jax.ShapeDtypeStruct((batch, RET_HEADS, RET_DK, RET_DV), F32)],
        compiler_params=_cparams("parallel", "arbitrary"),
        name="ret_prompt",
    )(proj, decay, xi, zeta, g_c)


def _ret_sample_kernel(bb, layer, first, qk_ref, vg_ref, qkt_ref, s0_ref, xi_ref, gc_ref, *rest):
    o_ref, s_ref = rest[-2:]
    hk = RET_HEADS * RET_DK
    hv = RET_HEADS * RET_DV
    n = qkt_ref.shape[2]
    scale = RET_DK ** -0.5
    seq_id = lax.broadcasted_iota(jnp.int32, (n, LANES), 0)
    widen = lambda col: jnp.concatenate([col] * (RET_DV // LANES), axis=1)
    for i in range(bb):
        pick = jnp.where(seq_id == pl.program_id(0) * bb + i, 1.0, 0.0).astype(BF16)
        qk_col = (jnp.dot(qkt_ref[0], pick, preferred_element_type=F32)
                  + jnp.dot(qkt_ref[1], pick, preferred_element_type=F32))
        for h in range(RET_HEADS):
            q_row = qk_ref[i:i + 1, h * RET_DK:(h + 1) * RET_DK] * scale
            k_row = qk_ref[i:i + 1, hk + h * RET_DK:hk + (h + 1) * RET_DK]
            v_row = vg_ref[i:i + 1, h * RET_DV:(h + 1) * RET_DV]
            g_row = vg_ref[i:i + 1, hv + h * RET_DV:hv + (h + 1) * RET_DV]
            q_col = widen(qk_col[h * RET_DK:(h + 1) * RET_DK] * scale)
            k_col = widen(qk_col[hk + h * RET_DK:hk + (h + 1) * RET_DK])
            s0 = s0_ref[i, h]
            a = jnp.sum(q_row * k_row, axis=-1, keepdims=True)
            qs = jnp.sum(q_col * s0, axis=0, keepdims=True)
            o = a * v_row + qs * xi_ref[h]
            s_new = gc_ref[h] * s0 + k_col * v_row
            if first:
                s_ref[layer, i, h] = s_new
            else:
                s_ref[i, h] = s_new
            o_ref[i:i + 1, h * RET_DV:(h + 1) * RET_DV] = _group_norm_gate(o, g_row).astype(o_ref.dtype)
    if first:
        for other in range(s_ref.shape[0]):
            if other != layer:
                s_ref[other] = jnp.zeros(s_ref.shape[1:], F32)


def ret_sample(proj, state, layer, new_state=None, bb=2):
    n = proj.shape[0]
    hk = RET_HEADS * RET_DK
    hv = RET_HEADS * RET_DV
    first = new_state is None
    _, xi, _, g_c = _ret_tables(1)
    qk = proj[:, :2 * hk]
    vg = proj[:, 2 * hk:]
    layer_block = (None, bb, RET_HEADS, RET_DK, RET_DV)
    layer_spec = pl.BlockSpec(layer_block, lambda i: (layer, i, 0, 0, 0))
    stack_spec = pl.BlockSpec((state.shape[0],) + layer_block[1:], lambda i: (0, i, 0, 0, 0))
    rows = lambda width: pl.BlockSpec((None, bb, width), lambda i: (i, 0, 0))
    qk_hi = qk.astype(BF16)
    qk_lo = (qk - qk_hi.astype(F32)).astype(BF16)
    qkt = jnp.stack([qk_hi.T, qk_lo.T])
    assert n == LANES
    in_specs = [rows(2 * hk), rows(2 * hv), _resident(qkt.shape), layer_spec,
                _resident((RET_HEADS, 1, 1)), _resident((RET_HEADS, 1, 1))]
    args = [qk.reshape(n // bb, bb, 2 * hk), vg.reshape(n // bb, bb, 2 * hv), qkt, state, xi, g_c]
    aliases = {}
    if not first:
        in_specs.append(pl.BlockSpec(memory_space=pl.ANY))
        args.append(new_state)
        aliases = {len(args) - 1: 1}
    o, s = pl.pallas_call(
        functools.partial(_ret_sample_kernel, bb, layer, first),
        grid=(n // bb,),
        in_specs=in_specs,
        out_specs=[rows(hv), stack_spec if first else layer_spec],
        out_shape=[jax.ShapeDtypeStruct((n // bb, bb, hv), BF16),
                   jax.ShapeDtypeStruct(state.shape, F32)],
        input_output_aliases=aliases,
        compiler_params=_cparams("parallel"),
        name="ret_sample",
    )(*args)
    return o.reshape(n, hv), s


def _pool_prompt_kernel(tl, x_ref, halo_ref, g0_ref, w_ref, sc_ref, g1_ref, o_ref, last_ref, buf_ref):
    i = pl.program_id(1)
    x = x_ref[...]
    xn = _rms(x, g0_ref[...])
    hn = _rms(halo_ref[...], g0_ref[...])
    buf_ref[0:POOL_HALO, :] = jnp.where(i > 0, hn, 0.0)
    buf_ref[POOL_HALO:POOL_HALO + tl, :] = xn
    pos = i * tl + lax.broadcasted_iota(jnp.int32, (tl, 1), 0)
    ys = []
    for gi, w in enumerate(POOL_WINDOWS):
        cols = slice(gi * POOL_GW, (gi + 1) * POOL_GW)
        cur = xn[:, cols]
        acc = cur
        for j in range(1, w):
            acc = acc + buf_ref[POOL_HALO - j:POOL_HALO - j + tl, cols]
        cnt = jnp.minimum(pos + 1, w).astype(F32)
        z = acc / cnt - cur
        ys.append(jnp.dot(z.astype(BF16), w_ref[gi], preferred_element_type=F32))
    y = jnp.concatenate(ys, axis=1) * sc_ref[...]
    o_ref[...] = x + _rms(y, g1_ref[...])
    last_ref[...] = buf_ref[tl:tl + POOL_HALO, :]


def pool_prompt(x, g0, w_grp, layer, scale, g1, tl=512):
    batch, seq, d = x.shape
    hb = tl // POOL_HALO
    vec = pl.BlockSpec((1, d), lambda b, i: (0, 0))
    return pl.pallas_call(
        functools.partial(_pool_prompt_kernel, tl),
        grid=(batch, seq // tl),
        in_specs=[pl.BlockSpec((None, tl, d), lambda b, i: (b, i, 0)),
                  pl.BlockSpec((None, POOL_HALO, d), lambda b, i: (b, jnp.maximum(i * hb - 1, 0), 0)),
                  vec,
                  _layer_resident(w_grp, layer),
                  vec, vec],
        out_specs=[pl.BlockSpec((None, tl, d), lambda b, i: (b, i, 0)),
                   pl.BlockSpec((None, POOL_HALO, d), lambda b, i: (b, 0, 0))],
        out_shape=[jax.ShapeDtypeStruct(x.shape, F32),
                   jax.ShapeDtypeStruct((batch, POOL_HALO, d), F32)],
        scratch_shapes=[pltpu.VMEM((POOL_HALO + tl, d), F32)],
        compiler_params=_cparams("parallel", "arbitrary"),
        name="pool_prompt",
    )(x, x, g0, w_grp, scale, g1)


def _pool_sample_kernel(x_ref, st_ref, g0_ref, w_ref, sc_ref, g1_ref, o_ref, nst_ref):
    d = D_MODEL
    x = x_ref[...]
    xn = _rms(x, g0_ref[...])
    ys = []
    for gi, w in enumerate(POOL_WINDOWS):
        acc = xn[:, gi * POOL_GW:(gi + 1) * POOL_GW]
        cur = acc
        for j in range(1, w):
            base = (POOL_PREV - j) * d + gi * POOL_GW
            acc = acc + st_ref[:, base:base + POOL_GW]
        cnt = float(min(PAST_LEN + 1, w))
        z = acc / cnt - cur
        ys.append(jnp.dot(z.astype(BF16), w_ref[gi], preferred_element_type=F32))
    y = jnp.concatenate(ys, axis=1) * sc_ref[...]
    o_ref[...] = x + _rms(y, g1_ref[...])
    nst_ref[:, :(POOL_PREV - 1) * d] = st_ref[:, d:]
    nst_ref[:, (POOL_PREV - 1) * d:] = xn


def pool_sample(x, state, g0, w_grp, layer, scale, g1, tb=32):
    n, d = x.shape
    vec = pl.BlockSpec((1, d), lambda i: (0, 0))
    return pl.pallas_call(
        _pool_sample_kernel,
        grid=(n // tb,),
        in_specs=[pl.BlockSpec((tb, d), lambda i: (i, 0)),
                  pl.BlockSpec((tb, POOL_PREV * d), lambda i: (i, 0)),
                  vec,
                  _layer_resident(w_grp, layer),
                  vec, vec],
        out_specs=[pl.BlockSpec((tb, d), lambda i: (i, 0)),
                   pl.BlockSpec((tb, POOL_PREV * d), lambda i: (i, 0))],
        out_shape=[jax.ShapeDtypeStruct((n, d), F32),
                   jax.ShapeDtypeStruct(state.shape, F32)],
        compiler_params=_cparams("parallel"),
        name="pool_sample",
    )(x, state, g0, w_grp, scale, g1)


def _attn_proj_kernel(dil, tn, x_ref, g_ref, w_ref, perm_ref, kv_ref, res_ref, tmp_ref):
    rows, kv_cols = kv_ref.shape
    t = x_ref.shape[0]
    nq = w_ref.shape[1] - kv_cols
    xn = _rms(x_ref[...], g_ref[...]).astype(BF16)
    nchunks = w_ref.shape[1] // tn
    lane_tiles = tn // LANES

    d1 = min(dil, ATT_REGROUP_RADIX)
    d2 = dil // d1

    def regroup(j):
        for c in range(lane_tiles):
            slot = (j * lane_tiles + c) % res_ref.shape[0]
            lanes = slice(j * tn + c * LANES, j * tn + (c + 1) * LANES)
            if d2 == 1:
                for r in range(dil):
                    perm_ref[r, :, lanes] = res_ref[slot, pl.ds(r, t // dil, stride=dil), :].astype(BF16)
                continue
            for r1 in range(d1):
                tmp_ref[slot, r1 * (t // d1):(r1 + 1) * (t // d1), :] = res_ref[slot, pl.ds(r1, t // d1, stride=d1), :]
            for r1 in range(d1):
                for r2 in range(d2):
                    perm_ref[r2 * d1 + r1, :, lanes] = (
                        tmp_ref[slot, pl.ds(r1 * (t // d1) + r2, t // dil, stride=d2), :].astype(BF16))

    for j in range(nchunks):
        cols = slice(j * tn, (j + 1) * tn)
        res = jnp.dot(xn, w_ref[:, cols], preferred_element_type=F32)
        if dil == 1:
            perm_ref[0, :, cols] = res.astype(BF16)
        else:
            for c in range(lane_tiles):
                res_ref[(j * lane_tiles + c) % res_ref.shape[0]] = res[:, c * LANES:(c + 1) * LANES]
        if j * tn >= nq:
            kv_ref[:, j * tn - nq:(j + 1) * tn - nq] = res[t - rows:, :]
        if dil > 1 and j > 0:
            regroup(j - 1)
    if dil > 1:
        regroup(nchunks - 1)


def attn_proj(x, g, w, layer, gi, t=1024, tn=256):
    batch, seq, d = x.shape
    win, dil = ATT_GROUPS[gi]
    kv_rows = min(win, seq)
    nq = ATT_HEADS * ATT_DH
    tkv = min(t, kv_rows)
    first = (seq - kv_rows) // t
    kv_index = lambda b, i: (b, jnp.maximum(i - first, 0), 0)
    perm, kv = pl.pallas_call(
        functools.partial(_attn_proj_kernel, dil, tn),
        grid=(batch, seq // t),
        in_specs=[pl.BlockSpec((None, t, d), lambda b, i: (b, i, 0)),
                  _resident((1, d)),
                  pl.BlockSpec((None, d, ATT_GW), lambda b, i: (layer, 0, gi), pipeline_mode=pl.Buffered(1))],
        out_specs=[pl.BlockSpec((None, dil, t // dil, ATT_GW), lambda b, i: (b, 0, i, 0)),
                   pl.BlockSpec((None, tkv, ATT_GW - nq), kv_index)],
        out_shape=[jax.ShapeDtypeStruct((batch, dil, seq // dil, ATT_GW), BF16),
                   jax.ShapeDtypeStruct((batch, kv_rows, ATT_GW - nq), F32)],
        scratch_shapes=[pltpu.VMEM((2 * tn // LANES, t, LANES), F32)] * 2,
        compiler_params=_cparams("parallel", "arbitrary"),
        name="attn_proj_g%d" % gi,
    )(x, g, w)
    return perm.reshape(batch, seq, ATT_GW), kv


def _alibi_slope(group, head):
    n = len(ATT_GROUPS) * ATT_HEADS
    idx = jnp.full((ATT_BLK, ATT_BLK), group * ATT_HEADS + 1, F32) + head
    return jnp.exp2(-8.0 * idx / n)


def _attn_prompt_kernel(seq, *refs):
    ng = len(ATT_GROUPS)
    qkv = [refs[3 * g:3 * g + 3] for g in range(ng)]
    out_ref, o_s, m_s, l_s = refs[3 * ng:]
    hp = pl.program_id(1)
    lane = lax.broadcasted_iota(jnp.int32, (ATT_BLK, LANES), 1)
    first_head = lane < ATT_DH
    qi = lax.broadcasted_iota(jnp.int32, (ATT_BLK, ATT_BLK), 0)
    kj = lax.broadcasted_iota(jnp.int32, (ATT_BLK, ATT_BLK), 1)
    scale = ATT_DH ** -0.5
    nt = (((1,), (1,)), ((), ()))
    ones = jnp.ones((2 * ATT_BLK, LANES), BF16)

    for g, (win, dil) in enumerate(ATT_GROUPS):
        q_ref, k_ref, v_ref = qkv[g]
        n_keys = win // dil
        ls = seq // dil
        nb = ls // ATT_BLK
        d_cur = qi - kj
        d_prev = qi + ATT_BLK - kj
        bias_cur, bias_prev = [], []
        for hh in range(2):
            slope = _alibi_slope(g, (hp * 2 + hh).astype(F32))
            bias_cur.append(jnp.where(d_cur >= 0, -slope * (dil * d_cur).astype(F32), NEG_INF))
            bias_prev.append(jnp.where(d_prev <= n_keys, -slope * (dil * d_prev).astype(F32), NEG_INF))
        bias_cur = jnp.concatenate(bias_cur, axis=0)
        bias_prev = jnp.concatenate(bias_prev, axis=0)
        rows = lambda qb: slice(qb * ATT_BLK, (qb + 1) * ATT_BLK)

        bias_both = jnp.concatenate([bias_prev, bias_cur], axis=1)

        def keys(qb):
            return slice((qb - 1) * ATT_BLK, (qb + 1) * ATT_BLK) if qb % nb > 0 else rows(qb)

        for c0 in range(0, seq // ATT_BLK, ATT_INTERLEAVE):
            blocks = range(c0, c0 + ATT_INTERLEAVE)
            qm = {}
            for qb in blocks:
                qf = q_ref[rows(qb), :].astype(F32) * scale
                qm[qb] = jnp.concatenate([jnp.where(first_head, qf, 0.0), jnp.where(first_head, 0.0, qf)],
                                         axis=0).astype(BF16)
            s = {qb: lax.dot_general(qm[qb], k_ref[keys(qb), :], nt, preferred_element_type=F32)
                 + (bias_both if qb % nb > 0 else bias_cur) for qb in blocks}
            halves = (slice(0, ATT_BLK), slice(ATT_BLK, 2 * ATT_BLK))
            m = {(qb, hh): jnp.max(s[qb][half], axis=-1, keepdims=True)
                 for qb in blocks for hh, half in enumerate(halves)}
            p = {qb: jnp.concatenate([jnp.exp(s[qb][half] - m[qb, hh]) for hh, half in enumerate(halves)],
                                     axis=0).astype(BF16) for qb in blocks}
            for qb in blocks:
                v = v_ref[keys(qb), :]
                acc = jnp.dot(p[qb], jnp.concatenate([v, ones[:v.shape[0]]], axis=1), preferred_element_type=F32)
                blk, res = qb % nb, qb // nb
                dst = pl.ds(blk * ATT_BLK * dil + res, ATT_BLK, stride=dil) if dil > 1 else rows(qb)
                o_s[g, dst, :] = jnp.where(first_head, acc[:ATT_BLK, :LANES], acc[ATT_BLK:, :LANES])
                l_s[g, dst, :] = jnp.where(first_head, acc[:ATT_BLK, LANES:], acc[ATT_BLK:, LANES:])
                m_s[g, dst, :] = jnp.where(first_head, m[qb, 0], m[qb, 1])

    m_all = m_s[0]
    for g in range(1, ng):
        m_all = jnp.maximum(m_all, m_s[g])
    num = jnp.zeros((seq, LANES), F32)
    den = jnp.zeros((seq, LANES), F32)
    for g in range(ng):
        e = jnp.exp(m_s[g] - m_all)
        num = num + e * o_s[g]
        den = den + e * l_s[g]
    out_ref[...] = (num / den).astype(out_ref.dtype)


def attn_prompt(qkv_groups, batch, seq):
    ng = len(ATT_GROUPS)
    hpairs = ATT_HEADS * ATT_DH // LANES
    in_specs, args = [], []
    for g in range(ng):
        for part in range(3):
            in_specs.append(pl.BlockSpec((seq, LANES), lambda b, h, part=part: (b, part * hpairs + h)))
            args.append(qkv_groups[g])
    return pl.pallas_call(
        functools.partial(_attn_prompt_kernel, seq),
        grid=(batch, hpairs),
        in_specs=in_specs,
        out_specs=pl.BlockSpec((seq, LANES), lambda b, h: (b, h)),
        out_shape=jax.ShapeDtypeStruct((batch * seq, D_MODEL), BF16),
        scratch_shapes=[pltpu.VMEM((ng, seq, LANES), F32)] * 3,
        compiler_params=_cparams("parallel", "parallel"),
        name="attn_prompt",
    )(*args)


def _attn_sample_kernel(hb, qkv_ref, slope_ref, c0_ref, c1_ref, c2_ref, o_ref):
    hg = pl.program_id(0)
    b = pl.program_id(1)
    caches = (c0_ref, c1_ref, c2_ref)
    ng = len(ATT_GROUPS)
    scale = ATT_DH ** -0.5

    @pl.when(b == 0)
    def _():
        o_ref[...] = jnp.zeros_like(o_ref)

    this_seq = lax.broadcasted_iota(jnp.int32, (ATT_DH, LANES), 1) == b
    pick = jnp.where(lax.broadcasted_iota(jnp.int32, (LANES, LANES), 0) == b, 1.0, 0.0).astype(BF16)
    lane_b = [[jnp.dot(qkv_ref[0, g, part], pick, preferred_element_type=F32)
               + jnp.dot(qkv_ref[1, g, part], pick, preferred_element_type=F32)
               for part in range(3)] for g in range(ng)]

    for hh in range(hb):
        rows = slice(hh * ATT_DH, (hh + 1) * ATT_DH)
        cols = [[lane_b[g][part][rows] for part in range(3)] for g in range(ng)]
        scores, new_scores = [], []
        m = None
        for g, (_, dil) in enumerate(ATT_GROUPS):
            wb = caches[g].shape[-1]
            qc = cols[g][0] * scale
            s = jnp.sum(caches[g][0, hh] * jnp.concatenate([qc] * (wb // LANES), axis=1),
                        axis=0, keepdims=True)
            pos = lax.broadcasted_iota(jnp.int32, (1, wb), 1)
            slope = slope_ref[g, hg * hb + hh]
            s = jnp.where((pos & (dil - 1)) == 0, s - slope * (wb - pos).astype(F32), NEG_INF)
            s_new = jnp.sum(qc * cols[g][1], axis=0, keepdims=True)[:, :1]
            mg = jnp.maximum(jnp.max(s, axis=1, keepdims=True), s_new)
            m = mg if m is None else jnp.maximum(m, mg)
            scores.append(s)
            new_scores.append(s_new)
        l = jnp.zeros((1, 1), F32)
        o = jnp.zeros((ATT_DH, 1), F32)
        for g in range(ng):
            p = jnp.exp(scores[g] - m)
            p_new = jnp.exp(new_scores[g] - m)
            l = l + jnp.sum(p, axis=1, keepdims=True) + p_new
            o = o + jnp.sum(caches[g][1, hh] * p, axis=1, keepdims=True) + p_new * cols[g][2]
        o_ref[rows, :] = jnp.where(this_seq, o / l, o_ref[rows, :])


def attn_sample(proj, caches, layer, hb=8):
    n = proj.shape[0]
    ng = len(ATT_GROUPS)
    hd = ATT_HEADS * ATT_DH
    assert n == LANES
    for c, (win, dil) in zip(caches, ATT_GROUPS):
        assert c.shape[2] == win and win % dil == 0 and dil & (dil - 1) == 0
    nall = ng * ATT_HEADS
    slopes = jnp.exp2(-8.0 * jnp.arange(1, nall + 1, dtype=F32) / nall).reshape(ng, ATT_HEADS)
    proj_hi = proj.astype(BF16)
    proj_lo = (proj - proj_hi.astype(F32)).astype(BF16)
    qkv_t = jnp.stack([proj_hi.T, proj_lo.T]).reshape(2, ng, 3, hd, n)
    cache_t = [jnp.transpose(c, (0, 1, 3, 4, 5, 2)) for c in caches]
    cache_specs = [pl.BlockSpec((None, None, 2, hb, ATT_DH, c.shape[-1]), lambda h, b: (layer, b, 0, h, 0, 0))
                   for c in cache_t]
    out_t = pl.pallas_call(
        functools.partial(_attn_sample_kernel, hb),
        grid=(ATT_HEADS // hb, n),
        in_specs=[pl.BlockSpec((2, ng, 3, hb * ATT_DH, n), lambda h, b: (0, 0, 0, h, 0)),
                  pl.BlockSpec(memory_space=pltpu.SMEM)] + cache_specs,
        out_specs=pl.BlockSpec((hb * ATT_DH, n), lambda h, b: (h, 0)),
        out_shape=jax.ShapeDtypeStruct((hd, n), F32),
        compiler_params=_cparams("parallel", "arbitrary"),
        name="attn_sample",
    )(qkv_t, slopes, *cache_t)
    return out_t.T


def kernel(x_prompt, x_sample, state_ret, state_pool, cache_kv_g0, cache_kv_g1, cache_kv_g2, norm_gains,
           w_in_ret, w_out_ret, w_grp_pool, scale_pool, w_in_attn, w_out_attn, w_ffn_in, w_ffn_out):
    batch, seq, d = x_prompt.shape
    ns = x_sample.shape[0]
    caches = (cache_kv_g0, cache_kv_g1, cache_kv_g2)
    ng = len(ATT_GROUPS)
    xp = x_prompt.reshape(batch * seq, d)
    xs = x_sample.reshape(ns, d)
    ret_p, pool_p, pool_s = [], [], []
    ret_s = None
    kv_p = [[] for _ in range(ng)]
    kv_s = [[] for _ in range(ng)]

    w_in_ret, w_out_ret, w_grp_pool, w_in_attn, w_out_attn, w_ffn_in, w_ffn_out = (
        w.astype(BF16) for w in (w_in_ret, w_out_ret, w_grp_pool, w_in_attn, w_out_attn, w_ffn_in, w_ffn_out))

    for i in range(DEPTH):
        kind, j = i % N_MIXERS, i // N_MIXERS
        g = [norm_gains[i, r][None, :] for r in range(4)]
        if kind == 0:
            proj = norm_matmul(xp, g[0], w_in_ret, j, BF16, 512)
            o, sp = ret_prompt(proj, batch, seq)
            xp = matmul_norm_res(o, w_out_ret, j, g[1], xp, 512)
            proj_s = norm_matmul(xs, g[0], w_in_ret, j, F32, ns)
            o_s, ret_s = ret_sample(proj_s, state_ret, j, ret_s)
            xs = matmul_norm_res(o_s, w_out_ret, j, g[1], xs, ns)
            ret_p.append(sp)
        elif kind == 1:
            scale = scale_pool[j][None, :]
            xp3, last = pool_prompt(xp.reshape(batch, seq, d), g[0], w_grp_pool, j, scale, g[1])
            xp = xp3.reshape(batch * seq, d)
            xs, nst = pool_sample(xs, state_pool[j].reshape(ns, POOL_PREV * d), g[0], w_grp_pool, j, scale, g[1])
            pool_p.append(last[:, POOL_HALO - POOL_PREV:])
            pool_s.append(nst.reshape(ns, POOL_PREV, d))
        else:
            x3 = xp.reshape(batch, seq, d)
            qkv = []
            for gi in range(ng):
                perm, kv = attn_proj(x3, g[0], w_in_attn, j, gi)
                qkv.append(perm.reshape(batch * seq, ATT_GW))
                kv_p[gi].append(kv.reshape(batch, kv.shape[1], 2, ATT_HEADS, ATT_DH))
            a = attn_prompt(qkv, batch, seq)
            xp = matmul_norm_res(a, w_out_attn, j, g[1], xp, 512)
            proj_s = norm_matmul(xs, g[0], w_in_attn, j, F32, ns)
            a_s = attn_sample(proj_s, caches, j)
            xs = matmul_norm_res(a_s.astype(BF16), w_out_attn, j, g[1], xs, ns)
            p5 = proj_s.reshape(ns, 1, ng, 3, ATT_HEADS, ATT_DH)
            for gi in range(ng):
                kv_s[gi].append(p5[:, :, gi, 1:])
        xp = ffn(xp, g[2], w_ffn_in, w_ffn_out, i, g[3], 1024)
        xs = ffn(xs, g[2], w_ffn_in, w_ffn_out, i, g[3], ns)

    return (xp.reshape(batch, seq, d), xs.reshape(ns, 1, d),
            jnp.stack(ret_p), ret_s,
            jnp.stack(pool_p), jnp.stack(pool_s),
            jnp.stack(kv_p[0]), jnp.stack(kv_s[0]),
            jnp.stack(kv_p[1]), jnp.stack(kv_s[1]),
            jnp.stack(kv_p[2]), jnp.stack(kv_s[2]))
```

```python
import functools

import jax
import jax.numpy as jnp
from jax import lax
from jax.experimental import pallas as pl
from jax.experimental.pallas import tpu as pltpu

F32 = jnp.float32
BF16 = jnp.bfloat16

D_MODEL = 1024
DEPTH = 4
PAST_LEN = 2048
N_MIXERS = 3

RET_HEADS = 4
RET_DK = D_MODEL // RET_HEADS
RET_DV = 2 * D_MODEL // RET_HEADS
RET_PROMPT_CHUNK = 256

POOL_WINDOWS = (2, 4, 8, 16)
POOL_GW = D_MODEL // len(POOL_WINDOWS)
POOL_PREV = max(POOL_WINDOWS) - 1
POOL_HALO = POOL_PREV + 1
POOL_PAD = 8

ATT_GROUPS = ((128, 1), (512, 4), (2048, 16))
ATT_HEADS = 16
ATT_DH = D_MODEL // ATT_HEADS
ATT_GW = 3 * ATT_HEADS * ATT_DH
ATT_BLK = 128
ATT_REGROUP_RADIX = 4
ATT_INTERLEAVE = 4

FFN_HIDDEN = 2816

NORM_EPS = 1e-6
GN_EPS = 1e-5
NEG_INF = -1e30

LANES = 128
VMEM_LIMIT = 56 * 1024 * 1024


def _cparams(*sem):
    return pltpu.CompilerParams(dimension_semantics=sem, vmem_limit_bytes=VMEM_LIMIT)


def _rms(x, g):
    return x * lax.rsqrt(jnp.mean(x * x, axis=-1, keepdims=True) + NORM_EPS) * g


def _silu(x):
    return x * jax.nn.sigmoid(x)


def _resident(shape):
    return pl.BlockSpec(shape, lambda *_: (0,) * len(shape), pipeline_mode=pl.Buffered(1))


def _layer_resident(w, layer):
    return pl.BlockSpec((None,) + w.shape[1:], lambda *_: (layer,) + (0,) * (w.ndim - 1),
                        pipeline_mode=pl.Buffered(1))


def _norm_matmul_kernel(tn, x_ref, g_ref, w_ref, o_ref):
    xn = _rms(x_ref[...], g_ref[...]).astype(BF16)
    for j in range(w_ref.shape[1] // tn):
        cols = slice(j * tn, (j + 1) * tn)
        o_ref[:, cols] = jnp.dot(xn, w_ref[:, cols], preferred_element_type=F32).astype(o_ref.dtype)


def norm_matmul(x, g, w, layer, out_dtype, tm, tn=512):
    m, d = x.shape
    n = w.shape[2]
    return pl.pallas_call(
        functools.partial(_norm_matmul_kernel, tn),
        grid=(m // tm,),
        in_specs=[pl.BlockSpec((tm, d), lambda i: (i, 0)), _resident((1, d)), _layer_resident(w, layer)],
        out_specs=pl.BlockSpec((tm, n), lambda i: (i, 0)),
        out_shape=jax.ShapeDtypeStruct((m, n), out_dtype),
        compiler_params=_cparams("parallel"),
        name="norm_matmul",
    )(x, g, w)


def _ffn_kernel(th, mixer, x_ref, g2_ref, wi_ref, wo_ref, g3_ref, *rest):
    o_ref, a_ref = rest[-2:]
    x = x_ref[...]
    if mixer:
        m_ref, wm_ref, g1_ref = rest[:3]
        x = x + _rms(jnp.dot(m_ref[...], wm_ref[...], preferred_element_type=F32), g1_ref[...])
    xn = _rms(x, g2_ref[...]).astype(BF16)
    for j in range(FFN_HIDDEN // th):
        gate = jnp.dot(xn, wi_ref[:, j * th:(j + 1) * th], preferred_element_type=F32)
        up = jnp.dot(xn, wi_ref[:, FFN_HIDDEN + j * th:FFN_HIDDEN + (j + 1) * th], preferred_element_type=F32)
        a_ref[:, j * th:(j + 1) * th] = (_silu(gate) * up).astype(BF16)
    y = jnp.dot(a_ref[...], wo_ref[...], preferred_element_type=F32)
    o_ref[...] = x + _rms(y, g3_ref[...])


def ffn(x, g2, w_in, w_out, layer, g3, tm, th=256, mixer=None):
    m, d = x.shape
    in_specs = [pl.BlockSpec((tm, d), lambda i: (i, 0)),
                _resident((1, d)), _layer_resident(w_in, layer), _layer_resident(w_out, layer),
                _resident((1, d))]
    args = [x, g2, w_in, w_out, g3]
    if mixer is not None:
        mix, w_mix, mix_layer, g1 = mixer
        in_specs += [pl.BlockSpec((tm, mix.shape[1]), lambda i: (i, 0)),
                     _layer_resident(w_mix, mix_layer), _resident((1, d))]
        args += [mix, w_mix, g1]
    return pl.pallas_call(
        functools.partial(_ffn_kernel, th, mixer is not None),
        grid=(m // tm,),
        in_specs=in_specs,
        out_specs=pl.BlockSpec((tm, d), lambda i: (i, 0)),
        out_shape=jax.ShapeDtypeStruct((m, d), F32),
        scratch_shapes=[pltpu.VMEM((tm, FFN_HIDDEN), BF16)],
        compiler_params=_cparams("parallel"),
        name="ffn",
    )(*args)


def _ret_tables(c):
    lg = jnp.log1p(-jnp.exp2(-5.0 - jnp.arange(RET_HEADS, dtype=F32)))
    n = jnp.arange(c, dtype=F32)
    diff = n[:, None] - n[None, :]
    decay = jnp.where(diff[None] >= 0, jnp.exp(jnp.maximum(diff, 0.0)[None] * lg[:, None, None]), 0.0)
    xi = jnp.exp((n[:, None] + 1.0) * lg[None, :])
    zeta = jnp.exp((c - 1.0 - n)[:, None] * lg[None, :])
    g_c = jnp.exp(c * lg)
    return decay, xi.T[:, :, None], zeta.T[:, :, None], g_c[:, None, None]


def _group_norm_gate(o, gate):
    mu = jnp.mean(o, axis=-1, keepdims=True)
    oc = o - mu
    var = jnp.mean(oc * oc, axis=-1, keepdims=True)
    return oc * lax.rsqrt(var + GN_EPS) * _silu(gate)


def _ret_prompt_kernel(nsub, c, p_ref, decay_ref, xi_ref, zeta_ref, gc_ref, o_ref, s_ref):
    @pl.when(pl.program_id(1) == 0)
    def _():
        s_ref[...] = jnp.zeros_like(s_ref)

    hk = RET_HEADS * RET_DK
    hv = RET_HEADS * RET_DV
    nt = (((1,), (1,)), ((), ()))
    for h in range(RET_HEADS):
        s = s_ref[h]
        for cc in range(nsub):
            rows = slice(cc * c, (cc + 1) * c)
            q = p_ref[rows, h * RET_DK:(h + 1) * RET_DK] * (RET_DK ** -0.5)
            k = p_ref[rows, hk + h * RET_DK:hk + (h + 1) * RET_DK]
            v = p_ref[rows, 2 * hk + h * RET_DV:2 * hk + (h + 1) * RET_DV]
            gt = p_ref[rows, 2 * hk + hv + h * RET_DV:2 * hk + hv + (h + 1) * RET_DV]
            a = lax.dot_general(q, k, nt, preferred_element_type=F32) * decay_ref[h]
            o = (jnp.dot(a.astype(BF16), v, preferred_element_type=F32)
                 + jnp.dot(q, s.astype(BF16), preferred_element_type=F32) * xi_ref[h])
            kz = (k.astype(F32) * zeta_ref[h]).T.astype(BF16)
            s = gc_ref[h] * s + jnp.dot(kz, v, preferred_element_type=F32)
            o_ref[rows, h * RET_DV:(h + 1) * RET_DV] = _group_norm_gate(o, gt.astype(F32)).astype(o_ref.dtype)
        s_ref[h] = s


def ret_prompt(proj, batch, seq, chunk=RET_PROMPT_CHUNK, nsub=2):
    rows = nsub * chunk
    nstep = seq // rows
    decay, xi, zeta, g_c = _ret_tables(chunk)
    full = lambda a: pl.BlockSpec(a.shape, lambda b, i: (0,) * a.ndim)
    return pl.pallas_call(
        functools.partial(_ret_prompt_kernel, nsub, chunk),
        grid=(batch, nstep),
        in_specs=[pl.BlockSpec((rows, proj.shape[1]), lambda b, i: (b * nstep + i, 0)),
                  full(decay), full(xi), full(zeta), full(g_c)],
        out_specs=[pl.BlockSpec((rows, RET_HEADS * RET_DV), lambda b, i: (b * nstep + i, 0)),
                   pl.BlockSpec((None, RET_HEADS, RET_DK, RET_DV), lambda b, i: (b, 0, 0, 0))],
        out_shape=[jax.ShapeDtypeStruct((batch * seq, RET_HEADS * RET_DV), BF16),
                   jax.ShapeDtypeStruct((batch, RET_HEADS, RET_DK, RET_DV), F32)],
        compiler_params=_cparams("parallel", "arbitrary"),
        name="ret_prompt",
    )(proj, decay, xi, zeta, g_c)


def _ret_sample_kernel(bb, layer, first, qk_ref, vg_ref, qkt_ref, s0_ref, xi_ref, gc_ref, *rest):
    o_ref, s_ref = rest[-2:]
    hk = RET_HEADS * RET_DK
    hv = RET_HEADS * RET_DV
    n = qkt_ref.shape[2]
    scale = RET_DK ** -0.5
    seq_id = lax.broadcasted_iota(jnp.int32, (n, LANES), 0)
    widen = lambda col: jnp.concatenate([col] * (RET_DV // LANES), axis=1)
    for i in range(bb):
        pick = jnp.where(seq_id == pl.program_id(0) * bb + i, 1.0, 0.0).astype(BF16)
        qk_col = (jnp.dot(qkt_ref[0], pick, preferred_element_type=F32)
                  + jnp.dot(qkt_ref[1], pick, preferred_element_type=F32))
        for h in range(RET_HEADS):
            q_row = qk_ref[i:i + 1, h * RET_DK:(h + 1) * RET_DK] * scale
            k_row = qk_ref[i:i + 1, hk + h * RET_DK:hk + (h + 1) * RET_DK]
            v_row = vg_ref[i:i + 1, h * RET_DV:(h + 1) * RET_DV]
            g_row = vg_ref[i:i + 1, hv + h * RET_DV:hv + (h + 1) * RET_DV]
            q_col = widen(qk_col[h * RET_DK:(h + 1) * RET_DK] * scale)
            k_col = widen(qk_col[hk + h * RET_DK:hk + (h + 1) * RET_DK])
            s0 = s0_ref[i, h]
            a = jnp.sum(q_row * k_row, axis=-1, keepdims=True)
            qs = jnp.sum(q_col * s0, axis=0, keepdims=True)
            o = a * v_row + qs * xi_ref[h]
            s_new = gc_ref[h] * s0 + k_col * v_row
            if first:
                s_ref[layer, i, h] = s_new
            else:
                s_ref[i, h] = s_new
            o_ref[i:i + 1, h * RET_DV:(h + 1) * RET_DV] = _group_norm_gate(o, g_row).astype(o_ref.dtype)
    if first:
        for other in range(s_ref.shape[0]):
            if other != layer:
                s_ref[other] = jnp.zeros(s_ref.shape[1:], F32)


def ret_sample(proj, state, layer, new_state=None, bb=2):
    n = proj.shape[0]
    hk = RET_HEADS * RET_DK
    hv = RET_HEADS * RET_DV
    first = new_state is None
    _, xi, _, g_c = _ret_tables(1)
    qk = proj[:, :2 * hk]
    vg = proj[:, 2 * hk:]
    layer_block = (None, bb, RET_HEADS, RET_DK, RET_DV)
    layer_spec = pl.BlockSpec(layer_block, lambda i: (layer, i, 0, 0, 0))
    stack_spec = pl.BlockSpec((state.shape[0],) + layer_block[1:], lambda i: (0, i, 0, 0, 0))
    rows = lambda width: pl.BlockSpec((None, bb, width), lambda i: (i, 0, 0))
    qk_hi = qk.astype(BF16)
    qk_lo = (qk - qk_hi.astype(F32)).astype(BF16)
    qkt = jnp.stack([qk_hi.T, qk_lo.T])
    assert n == LANES
    in_specs = [rows(2 * hk), rows(2 * hv), _resident(qkt.shape), layer_spec,
                _resident((RET_HEADS, 1, 1)), _resident((RET_HEADS, 1, 1))]
    args = [qk.reshape(n // bb, bb, 2 * hk), vg.reshape(n // bb, bb, 2 * hv), qkt, state, xi, g_c]
    aliases = {}
    if not first:
        in_specs.append(pl.BlockSpec(memory_space=pl.ANY))
        args.append(new_state)
        aliases = {len(args) - 1: 1}
    o, s = pl.pallas_call(
        functools.partial(_ret_sample_kernel, bb, layer, first),
        grid=(n // bb,),
        in_specs=in_specs,
        out_specs=[rows(hv), stack_spec if first else layer_spec],
        out_shape=[jax.ShapeDtypeStruct((n // bb, bb, hv), BF16),
                   jax.ShapeDtypeStruct(state.shape, F32)],
        input_output_aliases=aliases,
        compiler_params=_cparams("parallel"),
        name="ret_sample",
    )(*args)
    return o.reshape(n, hv), s


def _pool_prompt_kernel(tl, x_ref, halo_ref, g0_ref, w_ref, sc_ref, g1_ref, o_ref, last_ref, lvl_ref):
    assert POOL_WINDOWS == tuple(2 ** (k + 1) for k in range(len(POOL_WINDOWS)))
    i = pl.program_id(1)
    d = x_ref.shape[1]
    n = POOL_HALO + tl
    x = x_ref[...]
    xn = _rms(x, g0_ref[...])
    hn = _rms(halo_ref[...], g0_ref[...])
    lvl_ref[:, 0:POOL_PAD, :] = jnp.zeros((lvl_ref.shape[0], POOL_PAD, d), F32)
    lvl_ref[0, POOL_PAD:POOL_PAD + POOL_HALO, :] = jnp.where(i > 0, hn, 0.0)
    lvl_ref[0, POOL_PAD + POOL_HALO:POOL_PAD + n, :] = xn
    pos = i * tl + lax.broadcasted_iota(jnp.int32, (tl, 1), 0)
    ys = []
    for gi, w in enumerate(POOL_WINDOWS):
        cols = slice(gi * POOL_GW, d)
        half = w // 2
        sums = (lvl_ref[gi, POOL_PAD:POOL_PAD + n, cols]
                + lvl_ref[gi, POOL_PAD - half:POOL_PAD - half + n, cols])
        if gi + 1 < len(POOL_WINDOWS):
            lvl_ref[gi + 1, POOL_PAD:POOL_PAD + n, cols] = sums
        cur = xn[:, gi * POOL_GW:(gi + 1) * POOL_GW]
        cnt = jnp.minimum(pos + 1, w).astype(F32)
        z = sums[POOL_HALO:, :POOL_GW] / cnt - cur
        ys.append(jnp.dot(z.astype(BF16), w_ref[gi], preferred_element_type=F32))
    y = jnp.concatenate(ys, axis=1) * sc_ref[...]
    o_ref[...] = x + _rms(y, g1_ref[...])
    last_ref[...] = lvl_ref[0, POOL_PAD + tl:POOL_PAD + n, :]


def pool_prompt(x, g0, w_grp, layer, scale, g1, tl=512):
    batch, seq, d = x.shape
    hb = tl // POOL_HALO
    vec = pl.BlockSpec((1, d), lambda b, i: (0, 0))
    return pl.pallas_call(
        functools.partial(_pool_prompt_kernel, tl),
        grid=(batch, seq // tl),
        in_specs=[pl.BlockSpec((None, tl, d), lambda b, i: (b, i, 0)),
                  pl.BlockSpec((None, POOL_HALO, d), lambda b, i: (b, jnp.maximum(i * hb - 1, 0), 0)),
                  vec,
                  _layer_resident(w_grp, layer),
                  vec, vec],
        out_specs=[pl.BlockSpec((None, tl, d), lambda b, i: (b, i, 0)),
                   pl.BlockSpec((None, POOL_HALO, d), lambda b, i: (b, 0, 0))],
        out_shape=[jax.ShapeDtypeStruct(x.shape, F32),
                   jax.ShapeDtypeStruct((batch, POOL_HALO, d), F32)],
        scratch_shapes=[pltpu.VMEM((len(POOL_WINDOWS), POOL_PAD + POOL_HALO + tl, d), F32)],
        compiler_params=_cparams("parallel", "arbitrary"),
        name="pool_prompt",
    )(x, x, g0, w_grp, scale, g1)


def _pool_sample_kernel(x_ref, st_ref, g0_ref, w_ref, sc_ref, g1_ref, o_ref, nst_ref):
    d = D_MODEL
    x = x_ref[...]
    xn = _rms(x, g0_ref[...])
    ys = []
    for gi, w in enumerate(POOL_WINDOWS):
        acc = xn[:, gi * POOL_GW:(gi + 1) * POOL_GW]
        cur = acc
        for j in range(1, w):
            base = (POOL_PREV - j) * d + gi * POOL_GW
            acc = acc + st_ref[:, base:base + POOL_GW]
        cnt = float(min(PAST_LEN + 1, w))
        z = acc / cnt - cur
        ys.append(jnp.dot(z.astype(BF16), w_ref[gi], preferred_element_type=F32))
    y = jnp.concatenate(ys, axis=1) * sc_ref[...]
    o_ref[...] = x + _rms(y, g1_ref[...])
    nst_ref[:, :(POOL_PREV - 1) * d] = st_ref[:, d:]
    nst_ref[:, (POOL_PREV - 1) * d:] = xn


def pool_sample(x, state, g0, w_grp, layer, scale, g1, tb=32):
    n, d = x.shape
    vec = pl.BlockSpec((1, d), lambda i: (0, 0))
    return pl.pallas_call(
        _pool_sample_kernel,
        grid=(n // tb,),
        in_specs=[pl.BlockSpec((tb, d), lambda i: (i, 0)),
                  pl.BlockSpec((tb, POOL_PREV * d), lambda i: (i, 0)),
                  vec,
                  _layer_resident(w_grp, layer),
                  vec, vec],
        out_specs=[pl.BlockSpec((tb, d), lambda i: (i, 0)),
                   pl.BlockSpec((tb, POOL_PREV * d), lambda i: (i, 0))],
        out_shape=[jax.ShapeDtypeStruct((n, d), F32),
                   jax.ShapeDtypeStruct(state.shape, F32)],
        compiler_params=_cparams("parallel"),
        name="pool_sample",
    )(x, state, g0, w_grp, scale, g1)


def _attn_proj_kernel(dil, tn, x_ref, g_ref, w_ref, perm_ref, kv_ref, res_ref, tmp_ref):
    rows, kv_cols = kv_ref.shape
    t = x_ref.shape[0]
    nq = w_ref.shape[1] - kv_cols
    xn = _rms(x_ref[...], g_ref[...]).astype(BF16)
    nchunks = w_ref.shape[1] // tn
    lane_tiles = tn // LANES

    d1 = min(dil, ATT_REGROUP_RADIX)
    d2 = dil // d1

    def regroup(j):
        for c in range(lane_tiles):
            slot = (j * lane_tiles + c) % res_ref.shape[0]
            lanes = slice(j * tn + c * LANES, j * tn + (c + 1) * LANES)
            if d2 == 1:
                for r in range(dil):
                    perm_ref[r, :, lanes] = res_ref[slot, pl.ds(r, t // dil, stride=dil), :].astype(BF16)
                continue
            for r1 in range(d1):
                tmp_ref[slot, r1 * (t // d1):(r1 + 1) * (t // d1), :] = res_ref[slot, pl.ds(r1, t // d1, stride=d1), :]
            for r1 in range(d1):
                for r2 in range(d2):
                    perm_ref[r2 * d1 + r1, :, lanes] = (
                        tmp_ref[slot, pl.ds(r1 * (t // d1) + r2, t // dil, stride=d2), :].astype(BF16))

    for j in range(nchunks):
        cols = slice(j * tn, (j + 1) * tn)
        res = jnp.dot(xn, w_ref[:, cols], preferred_element_type=F32)
        if dil == 1:
            perm_ref[0, :, cols] = res.astype(BF16)
        else:
            for c in range(lane_tiles):
                res_ref[(j * lane_tiles + c) % res_ref.shape[0]] = res[:, c * LANES:(c + 1) * LANES]
        if j * tn >= nq:
            kv_ref[:, j * tn - nq:(j + 1) * tn - nq] = res[t - rows:, :]
        if dil > 1 and j > 0:
            regroup(j - 1)
    if dil > 1:
        regroup(nchunks - 1)


def attn_proj(x, g, w, layer, gi, t=1024, tn=256):
    batch, seq, d = x.shape
    win, dil = ATT_GROUPS[gi]
    kv_rows = min(win, seq)
    nq = ATT_HEADS * ATT_DH
    tkv = min(t, kv_rows)
    first = (seq - kv_rows) // t
    kv_index = lambda b, i: (b, jnp.maximum(i - first, 0), 0)
    perm, kv = pl.pallas_call(
        functools.partial(_attn_proj_kernel, dil, tn),
        grid=(batch, seq // t),
        in_specs=[pl.BlockSpec((None, t, d), lambda b, i: (b, i, 0)),
                  _resident((1, d)),
                  pl.BlockSpec((None, d, ATT_GW), lambda b, i: (layer, 0, gi), pipeline_mode=pl.Buffered(1))],
        out_specs=[pl.BlockSpec((None, dil, t // dil, ATT_GW), lambda b, i: (b, 0, i, 0)),
                   pl.BlockSpec((None, tkv, ATT_GW - nq), kv_index)],
        out_shape=[jax.ShapeDtypeStruct((batch, dil, seq // dil, ATT_GW), BF16),
                   jax.ShapeDtypeStruct((batch, kv_rows, ATT_GW - nq), F32)],
        scratch_shapes=[pltpu.VMEM((2 * tn // LANES, t, LANES), F32)] * 2,
        compiler_params=_cparams("parallel", "arbitrary"),
        name="attn_proj_g%d" % gi,
    )(x, g, w)
    return perm.reshape(batch, seq, ATT_GW), kv


def _alibi_slope(group, head):
    n = len(ATT_GROUPS) * ATT_HEADS
    idx = jnp.full((ATT_BLK, ATT_BLK), group * ATT_HEADS + 1, F32) + head
    return jnp.exp2(-8.0 * idx / n)


def _attn_prompt_kernel(seq, *refs):
    ng = len(ATT_GROUPS)
    qkv = [refs[3 * g:3 * g + 3] for g in range(ng)]
    out_ref, o_s, m_s, l_s = refs[3 * ng:]
    hp = pl.program_id(1)
    lane = lax.broadcasted_iota(jnp.int32, (ATT_BLK, LANES), 1)
    first_head = lane < ATT_DH
    qi = lax.broadcasted_iota(jnp.int32, (ATT_BLK, ATT_BLK), 0)
    kj = lax.broadcasted_iota(jnp.int32, (ATT_BLK, ATT_BLK), 1)
    scale = ATT_DH ** -0.5
    nt = (((1,), (1,)), ((), ()))
    ones = jnp.ones((2 * ATT_BLK, LANES), BF16)

    for g, (win, dil) in enumerate(ATT_GROUPS):
        q_ref, k_ref, v_ref = qkv[g]
        n_keys = win // dil
        ls = seq // dil
        nb = ls // ATT_BLK
        d_cur = qi - kj
        d_prev = qi + ATT_BLK - kj
        bias_cur, bias_prev = [], []
        for hh in range(2):
            slope = _alibi_slope(g, (hp * 2 + hh).astype(F32))
            bias_cur.append(jnp.where(d_cur >= 0, -slope * (dil * d_cur).astype(F32), NEG_INF))
            bias_prev.append(jnp.where(d_prev <= n_keys, -slope * (dil * d_prev).astype(F32), NEG_INF))
        bias_cur = jnp.concatenate(bias_cur, axis=0)
        bias_prev = jnp.concatenate(bias_prev, axis=0)
        rows = lambda qb: slice(qb * ATT_BLK, (qb + 1) * ATT_BLK)

        bias_both = jnp.concatenate([bias_prev, bias_cur], axis=1)

        def keys(qb):
            return slice((qb - 1) * ATT_BLK, (qb + 1) * ATT_BLK) if qb % nb > 0 else rows(qb)

        for c0 in range(0, seq // ATT_BLK, ATT_INTERLEAVE):
            blocks = range(c0, c0 + ATT_INTERLEAVE)
            qm = {}
            for qb in blocks:
                qf = q_ref[rows(qb), :].astype(F32) * scale
                qm[qb] = jnp.concatenate([jnp.where(first_head, qf, 0.0), jnp.where(first_head, 0.0, qf)],
                                         axis=0).astype(BF16)
            s = {qb: lax.dot_general(qm[qb], k_ref[keys(qb), :], nt, preferred_element_type=F32)
                 + (bias_both if qb % nb > 0 else bias_cur) for qb in blocks}
            halves = (slice(0, ATT_BLK), slice(ATT_BLK, 2 * ATT_BLK))
            m = {(qb, hh): jnp.max(s[qb][half], axis=-1, keepdims=True)
                 for qb in blocks for hh, half in enumerate(halves)}
            p = {qb: jnp.concatenate([jnp.exp(s[qb][half] - m[qb, hh]) for hh, half in enumerate(halves)],
                                     axis=0).astype(BF16) for qb in blocks}
            for qb in blocks:
                v = v_ref[keys(qb), :]
                acc = jnp.dot(p[qb], jnp.concatenate([v, ones[:v.shape[0]]], axis=1), preferred_element_type=F32)
                blk, res = qb % nb, qb // nb
                dst = pl.ds(blk * ATT_BLK * dil + res, ATT_BLK, stride=dil) if dil > 1 else rows(qb)
                o_s[g, dst, :] = jnp.where(first_head, acc[:ATT_BLK, :LANES], acc[ATT_BLK:, :LANES])
                l_s[g, dst, :] = jnp.where(first_head, acc[:ATT_BLK, LANES:], acc[ATT_BLK:, LANES:])
                m_s[g, dst, :] = jnp.where(first_head, m[qb, 0], m[qb, 1])

    m_all = m_s[0]
    for g in range(1, ng):
        m_all = jnp.maximum(m_all, m_s[g])
    num = jnp.zeros((seq, LANES), F32)
    den = jnp.zeros((seq, LANES), F32)
    for g in range(ng):
        e = jnp.exp(m_s[g] - m_all)
        num = num + e * o_s[g]
        den = den + e * l_s[g]
    out_ref[...] = (num / den).astype(out_ref.dtype)


def attn_prompt(qkv_groups, batch, seq):
    ng = len(ATT_GROUPS)
    hpairs = ATT_HEADS * ATT_DH // LANES
    in_specs, args = [], []
    for g in range(ng):
        for part in range(3):
            in_specs.append(pl.BlockSpec((seq, LANES), lambda b, h, part=part: (b, part * hpairs + h)))
            args.append(qkv_groups[g])
    return pl.pallas_call(
        functools.partial(_attn_prompt_kernel, seq),
        grid=(batch, hpairs),
        in_specs=in_specs,
        out_specs=pl.BlockSpec((seq, LANES), lambda b, h: (b, h)),
        out_shape=jax.ShapeDtypeStruct((batch * seq, D_MODEL), BF16),
        scratch_shapes=[pltpu.VMEM((ng, seq, LANES), F32)] * 3,
        compiler_params=_cparams("parallel", "parallel"),
        name="attn_prompt",
    )(*args)


def _attn_sample_kernel(hb, qkv_ref, slope_ref, c0_ref, c1_ref, c2_ref, o_ref):
    hg = pl.program_id(0)
    b = pl.program_id(1)
    caches = (c0_ref, c1_ref, c2_ref)
    ng = len(ATT_GROUPS)
    scale = ATT_DH ** -0.5

    @pl.when(b == 0)
    def _():
        o_ref[...] = jnp.zeros_like(o_ref)

    this_seq = lax.broadcasted_iota(jnp.int32, (ATT_DH, LANES), 1) == b
    pick = jnp.where(lax.broadcasted_iota(jnp.int32, (LANES, LANES), 0) == b, 1.0, 0.0).astype(BF16)
    lane_b = [[jnp.dot(qkv_ref[0, g, part], pick, preferred_element_type=F32)
               + jnp.dot(qkv_ref[1, g, part], pick, preferred_element_type=F32)
               for part in range(3)] for g in range(ng)]

    for hh in range(hb):
        rows = slice(hh * ATT_DH, (hh + 1) * ATT_DH)
        cols = [[lane_b[g][part][rows] for part in range(3)] for g in range(ng)]
        scores, new_scores = [], []
        m = None
        for g, (_, dil) in enumerate(ATT_GROUPS):
            wb = caches[g].shape[-1]
            qc = cols[g][0] * scale
            s = jnp.sum(caches[g][0, hh] * jnp.concatenate([qc] * (wb // LANES), axis=1),
                        axis=0, keepdims=True)
            pos = lax.broadcasted_iota(jnp.int32, (1, wb), 1)
            slope = slope_ref[g, hg * hb + hh]
            s = jnp.where((pos & (dil - 1)) == 0, s - slope * (wb - pos).astype(F32), NEG_INF)
            s_new = jnp.sum(qc * cols[g][1], axis=0, keepdims=True)[:, :1]
            mg = jnp.maximum(jnp.max(s, axis=1, keepdims=True), s_new)
            m = mg if m is None else jnp.maximum(m, mg)
            scores.append(s)
            new_scores.append(s_new)
        l = jnp.zeros((1, 1), F32)
        o = jnp.zeros((ATT_DH, 1), F32)
        for g in range(ng):
            p = jnp.exp(scores[g] - m)
            p_new = jnp.exp(new_scores[g] - m)
            l = l + jnp.sum(p, axis=1, keepdims=True) + p_new
            o = o + jnp.sum(caches[g][1, hh] * p, axis=1, keepdims=True) + p_new * cols[g][2]
        o_ref[rows, :] = jnp.where(this_seq, o / l, o_ref[rows, :])


def attn_sample(proj, caches, layer, hb=8):
    n = proj.shape[0]
    ng = len(ATT_GROUPS)
    hd = ATT_HEADS * ATT_DH
    assert n == LANES
    for c, (win, dil) in zip(caches, ATT_GROUPS):
        assert c.shape[2] == win and win % dil == 0 and dil & (dil - 1) == 0
    nall = ng * ATT_HEADS
    slopes = jnp.exp2(-8.0 * jnp.arange(1, nall + 1, dtype=F32) / nall).reshape(ng, ATT_HEADS)
    proj_hi = proj.astype(BF16)
    proj_lo = (proj - proj_hi.astype(F32)).astype(BF16)
    qkv_t = jnp.stack([proj_hi.T, proj_lo.T]).reshape(2, ng, 3, hd, n)
    cache_t = [jnp.transpose(c, (0, 1, 3, 4, 5, 2)) for c in caches]
    cache_specs = [pl.BlockSpec((None, None, 2, hb, ATT_DH, c.shape[-1]), lambda h, b: (layer, b, 0, h, 0, 0))
                   for c in cache_t]
    out_t = pl.pallas_call(
        functools.partial(_attn_sample_kernel, hb),
        grid=(ATT_HEADS // hb, n),
        in_specs=[pl.BlockSpec((2, ng, 3, hb * ATT_DH, n), lambda h, b: (0, 0, 0, h, 0)),
                  pl.BlockSpec(memory_space=pltpu.SMEM)] + cache_specs,
        out_specs=pl.BlockSpec((hb * ATT_DH, n), lambda h, b: (h, 0)),
        out_shape=jax.ShapeDtypeStruct((hd, n), F32),
        compiler_params=_cparams("parallel", "arbitrary"),
        name="attn_sample",
    )(qkv_t, slopes, *cache_t)
    return out_t.T


def kernel(x_prompt, x_sample, state_ret, state_pool, cache_kv_g0, cache_kv_g1, cache_kv_g2, norm_gains,
           w_in_ret, w_out_ret, w_grp_pool, scale_pool, w_in_attn, w_out_attn, w_ffn_in, w_ffn_out):
    batch, seq, d = x_prompt.shape
    ns = x_sample.shape[0]
    caches = (cache_kv_g0, cache_kv_g1, cache_kv_g2)
    ng = len(ATT_GROUPS)
    xp = x_prompt.reshape(batch * seq, d)
    xs = x_sample.reshape(ns, d)
    ret_p, pool_p, pool_s = [], [], []
    ret_s = None
    kv_p = [[] for _ in range(ng)]
    kv_s = [[] for _ in range(ng)]

    w_in_ret, w_out_ret, w_grp_pool, w_in_attn, w_out_attn, w_ffn_in, w_ffn_out = (
        w.astype(BF16) for w in (w_in_ret, w_out_ret, w_grp_pool, w_in_attn, w_out_attn, w_ffn_in, w_ffn_out))

    for i in range(DEPTH):
        kind, j = i % N_MIXERS, i // N_MIXERS
        g = [norm_gains[i, r][None, :] for r in range(4)]
        mix_p = mix_s = None
        if kind == 0:
            proj = norm_matmul(xp, g[0], w_in_ret, j, BF16, 512)
            o, sp = ret_prompt(proj, batch, seq)
            mix_p = (o, w_out_ret, j, g[1])
            proj_s = norm_matmul(xs, g[0], w_in_ret, j, F32, ns)
            o_s, ret_s = ret_sample(proj_s, state_ret, j, ret_s)
            mix_s = (o_s, w_out_ret, j, g[1])
            ret_p.append(sp)
        elif kind == 1:
            scale = scale_pool[j][None, :]
            xp3, last = pool_prompt(xp.reshape(batch, seq, d), g[0], w_grp_pool, j, scale, g[1])
            xp = xp3.reshape(batch * seq, d)
            xs, nst = pool_sample(xs, state_pool[j].reshape(ns, POOL_PREV * d), g[0], w_grp_pool, j, scale, g[1])
            pool_p.append(last[:, POOL_HALO - POOL_PREV:])
            pool_s.append(nst.reshape(ns, POOL_PREV, d))
        else:
            x3 = xp.reshape(batch, seq, d)
            qkv = []
            for gi in range(ng):
                perm, kv = attn_proj(x3, g[0], w_in_attn, j, gi)
                qkv.append(perm.reshape(batch * seq, ATT_GW))
                kv_p[gi].append(kv.reshape(batch, kv.shape[1], 2, ATT_HEADS, ATT_DH))
            mix_p = (attn_prompt(qkv, batch, seq), w_out_attn, j, g[1])
            proj_s = norm_matmul(xs, g[0], w_in_attn, j, F32, ns)
            mix_s = (attn_sample(proj_s, caches, j).astype(BF16), w_out_attn, j, g[1])
            p5 = proj_s.reshape(ns, 1, ng, 3, ATT_HEADS, ATT_DH)
            for gi in range(ng):
                kv_s[gi].append(p5[:, :, gi, 1:])
        xp = ffn(xp, g[2], w_ffn_in, w_ffn_out, i, g[3], 1024 if mix_p is None else 512, mixer=mix_p)
        xs = ffn(xs, g[2], w_ffn_in, w_ffn_out, i, g[3], ns, mixer=mix_s)

    return (xp.reshape(batch, seq, d), xs.reshape(ns, 1, d),
            jnp.stack(ret_p), ret_s,
            jnp.stack(pool_p), jnp.stack(pool_s),
            jnp.stack(kv_p[0]), jnp.stack(kv_s[0]),
            jnp.stack(kv_p[1]), jnp.stack(kv_s[1]),
            jnp.stack(kv_p[2]), jnp.stack(kv_s[2]))
```

```python
import functools

import jax
import jax.numpy as jnp
from jax import lax
from jax.experimental import pallas as pl
from jax.experimental.pallas import tpu as pltpu

F32 = jnp.float32
BF16 = jnp.bfloat16

D_MODEL = 1024
DEPTH = 4
PAST_LEN = 2048
N_MIXERS = 3

RET_HEADS = 4
RET_DK = D_MODEL // RET_HEADS
RET_DV = 2 * D_MODEL // RET_HEADS
RET_PROMPT_CHUNK = 256

POOL_WINDOWS = (2, 4, 8, 16)
POOL_GW = D_MODEL // len(POOL_WINDOWS)
POOL_PREV = max(POOL_WINDOWS) - 1
POOL_HALO = POOL_PREV + 1
POOL_PAD = 8

ATT_GROUPS = ((128, 1), (512, 4), (2048, 16))
ATT_HEADS = 16
ATT_DH = D_MODEL // ATT_HEADS
ATT_GW = 3 * ATT_HEADS * ATT_DH
ATT_BLK = 128
ATT_REGROUP_RADIX = 4
ATT_INTERLEAVE = 8

FFN_HIDDEN = 2816

NORM_EPS = 1e-6
GN_EPS = 1e-5
NEG_INF = -1e30
LOG2E = 1.4426950408889634
ATT_Q_SCALE = ATT_DH ** -0.5 * LOG2E

LANES = 128
VMEM_LIMIT = 56 * 1024 * 1024


def _cparams(*sem):
    return pltpu.CompilerParams(dimension_semantics=sem, vmem_limit_bytes=VMEM_LIMIT)


def _rms(x, g):
    return x * lax.rsqrt(jnp.mean(x * x, axis=-1, keepdims=True) + NORM_EPS) * g


def _silu(x):
    return x * jax.nn.sigmoid(x)


def _resident(shape):
    return pl.BlockSpec(shape, lambda *_: (0,) * len(shape), pipeline_mode=pl.Buffered(1))


def _layer_resident(w, layer):
    return pl.BlockSpec((None,) + w.shape[1:], lambda *_: (layer,) + (0,) * (w.ndim - 1),
                        pipeline_mode=pl.Buffered(1))


def _norm_matmul_kernel(tn, x_ref, g_ref, w_ref, o_ref):
    xn = _rms(x_ref[...], g_ref[...]).astype(BF16)
    for j in range(w_ref.shape[1] // tn):
        cols = slice(j * tn, (j + 1) * tn)
        o_ref[:, cols] = jnp.dot(xn, w_ref[:, cols], preferred_element_type=F32).astype(o_ref.dtype)


def norm_matmul(x, g, w, layer, out_dtype, tm, tn=512):
    m, d = x.shape
    n = w.shape[2]
    return pl.pallas_call(
        functools.partial(_norm_matmul_kernel, tn),
        grid=(m // tm,),
        in_specs=[pl.BlockSpec((tm, d), lambda i: (i, 0)), _resident((1, d)), _layer_resident(w, layer)],
        out_specs=pl.BlockSpec((tm, n), lambda i: (i, 0)),
        out_shape=jax.ShapeDtypeStruct((m, n), out_dtype),
        compiler_params=_cparams("parallel"),
        name="norm_matmul",
    )(x, g, w)


def _ffn_kernel(th, mixer, x_ref, g2_ref, wi_ref, wo_ref, g3_ref, *rest):
    o_ref, a_ref = rest[-2:]
    x = x_ref[...]
    if mixer:
        m_ref, wm_ref, g1_ref = rest[:3]
        x = x + _rms(jnp.dot(m_ref[...], wm_ref[...], preferred_element_type=F32), g1_ref[...])
    xn = _rms(x, g2_ref[...]).astype(BF16)
    for j in range(FFN_HIDDEN // th):
        gate = jnp.dot(xn, wi_ref[:, j * th:(j + 1) * th], preferred_element_type=F32)
        up = jnp.dot(xn, wi_ref[:, FFN_HIDDEN + j * th:FFN_HIDDEN + (j + 1) * th], preferred_element_type=F32)
        a_ref[:, j * th:(j + 1) * th] = (_silu(gate) * up).astype(BF16)
    y = jnp.dot(a_ref[...], wo_ref[...], preferred_element_type=F32)
    o_ref[...] = x + _rms(y, g3_ref[...])


def ffn(x, g2, w_in, w_out, layer, g3, tm, th=256, mixer=None):
    m, d = x.shape
    in_specs = [pl.BlockSpec((tm, d), lambda i: (i, 0)),
                _resident((1, d)), _layer_resident(w_in, layer), _layer_resident(w_out, layer),
                _resident((1, d))]
    args = [x, g2, w_in, w_out, g3]
    if mixer is not None:
        mix, w_mix, mix_layer, g1 = mixer
        in_specs += [pl.BlockSpec((tm, mix.shape[1]), lambda i: (i, 0)),
                     _layer_resident(w_mix, mix_layer), _resident((1, d))]
        args += [mix, w_mix, g1]
    return pl.pallas_call(
        functools.partial(_ffn_kernel, th, mixer is not None),
        grid=(m // tm,),
        in_specs=in_specs,
        out_specs=pl.BlockSpec((tm, d), lambda i: (i, 0)),
        out_shape=jax.ShapeDtypeStruct((m, d), F32),
        scratch_shapes=[pltpu.VMEM((tm, FFN_HIDDEN), BF16)],
        compiler_params=_cparams("parallel"),
        name="ffn",
    )(*args)


def _ret_tables(c):
    lg = jnp.log1p(-jnp.exp2(-5.0 - jnp.arange(RET_HEADS, dtype=F32)))
    n = jnp.arange(c, dtype=F32)
    diff = n[:, None] - n[None, :]
    decay = jnp.where(diff[None] >= 0, jnp.exp(jnp.maximum(diff, 0.0)[None] * lg[:, None, None]), 0.0)
    xi = jnp.exp((n[:, None] + 1.0) * lg[None, :])
    zeta = jnp.exp((c - 1.0 - n)[:, None] * lg[None, :])
    g_c = jnp.exp(c * lg)
    return decay, xi.T[:, :, None], zeta.T[:, :, None], g_c[:, None, None]


def _group_norm_gate(o, gate):
    mu = jnp.mean(o, axis=-1, keepdims=True)
    oc = o - mu
    var = jnp.mean(oc * oc, axis=-1, keepdims=True)
    return oc * lax.rsqrt(var + GN_EPS) * _silu(gate)


def _ret_prompt_kernel(nsub, c, p_ref, decay_ref, xi_ref, zeta_ref, gc_ref, o_ref, s_ref):
    @pl.when(pl.program_id(1) == 0)
    def _():
        s_ref[...] = jnp.zeros_like(s_ref)

    hk = RET_HEADS * RET_DK
    hv = RET_HEADS * RET_DV
    nt = (((1,), (1,)), ((), ()))
    for h in range(RET_HEADS):
        s = s_ref[h]
        for cc in range(nsub):
            rows = slice(cc * c, (cc + 1) * c)
            q = p_ref[rows, h * RET_DK:(h + 1) * RET_DK] * (RET_DK ** -0.5)
            k = p_ref[rows, hk + h * RET_DK:hk + (h + 1) * RET_DK]
            v = p_ref[rows, 2 * hk + h * RET_DV:2 * hk + (h + 1) * RET_DV]
            gt = p_ref[rows, 2 * hk + hv + h * RET_DV:2 * hk + hv + (h + 1) * RET_DV]
            a = lax.dot_general(q, k, nt, preferred_element_type=F32) * decay_ref[h]
            o = (jnp.dot(a.astype(BF16), v, preferred_element_type=F32)
                 + jnp.dot(q, s.astype(BF16), preferred_element_type=F32) * xi_ref[h])
            kz = (k.astype(F32) * zeta_ref[h]).T.astype(BF16)
            s = gc_ref[h] * s + jnp.dot(kz, v, preferred_element_type=F32)
            o_ref[rows, h * RET_DV:(h + 1) * RET_DV] = _group_norm_gate(o, gt.astype(F32)).astype(o_ref.dtype)
        s_ref[h] = s


def ret_prompt(proj, batch, seq, chunk=RET_PROMPT_CHUNK, nsub=2):
    rows = nsub * chunk
    nstep = seq // rows
    decay, xi, zeta, g_c = _ret_tables(chunk)
    full = lambda a: pl.BlockSpec(a.shape, lambda b, i: (0,) * a.ndim)
    return pl.pallas_call(
        functools.partial(_ret_prompt_kernel, nsub, chunk),
        grid=(batch, nstep),
        in_specs=[pl.BlockSpec((rows, proj.shape[1]), lambda b, i: (b * nstep + i, 0)),
                  full(decay), full(xi), full(zeta), full(g_c)],
        out_specs=[pl.BlockSpec((rows, RET_HEADS * RET_DV), lambda b, i: (b * nstep + i, 0)),
                   pl.BlockSpec((None, RET_HEADS, RET_DK, RET_DV), lambda b, i: (b, 0, 0, 0))],
        out_shape=[jax.ShapeDtypeStruct((batch * seq, RET_HEADS * RET_DV), BF16),
                   jax.ShapeDtypeStruct((batch, RET_HEADS, RET_DK, RET_DV), F32)],
        compiler_params=_cparams("parallel", "arbitrary"),
        name="ret_prompt",
    )(proj, decay, xi, zeta, g_c)


def _ret_sample_kernel(bb, layer, first, qk_ref, vg_ref, qkt_ref, s0_ref, xi_ref, gc_ref, *rest):
    o_ref, s_ref = rest[-2:]
    hk = RET_HEADS * RET_DK
    hv = RET_HEADS * RET_DV
    n = qkt_ref.shape[2]
    scale = RET_DK ** -0.5
    seq_id = lax.broadcasted_iota(jnp.int32, (n, LANES), 0)
    widen = lambda col: jnp.concatenate([col] * (RET_DV // LANES), axis=1)
    for i in range(bb):
        pick = jnp.where(seq_id == pl.program_id(0) * bb + i, 1.0, 0.0).astype(BF16)
        qk_col = (jnp.dot(qkt_ref[0], pick, preferred_element_type=F32)
                  + jnp.dot(qkt_ref[1], pick, preferred_element_type=F32))
        for h in range(RET_HEADS):
            q_row = qk_ref[i:i + 1, h * RET_DK:(h + 1) * RET_DK] * scale
            k_row = qk_ref[i:i + 1, hk + h * RET_DK:hk + (h + 1) * RET_DK]
            v_row = vg_ref[i:i + 1, h * RET_DV:(h + 1) * RET_DV]
            g_row = vg_ref[i:i + 1, hv + h * RET_DV:hv + (h + 1) * RET_DV]
            q_col = widen(qk_col[h * RET_DK:(h + 1) * RET_DK] * scale)
            k_col = widen(qk_col[hk + h * RET_DK:hk + (h + 1) * RET_DK])
            s0 = s0_ref[i, h]
            a = jnp.sum(q_row * k_row, axis=-1, keepdims=True)
            qs = jnp.sum(q_col * s0, axis=0, keepdims=True)
            o = a * v_row + qs * xi_ref[h]
            s_new = gc_ref[h] * s0 + k_col * v_row
            if first:
                s_ref[layer, i, h] = s_new
            else:
                s_ref[i, h] = s_new
            o_ref[i:i + 1, h * RET_DV:(h + 1) * RET_DV] = _group_norm_gate(o, g_row).astype(o_ref.dtype)
    if first:
        for other in range(s_ref.shape[0]):
            if other != layer:
                s_ref[other] = jnp.zeros(s_ref.shape[1:], F32)


def ret_sample(proj, state, layer, new_state=None, bb=2):
    n = proj.shape[0]
    hk = RET_HEADS * RET_DK
    hv = RET_HEADS * RET_DV
    first = new_state is None
    _, xi, _, g_c = _ret_tables(1)
    qk = proj[:, :2 * hk]
    vg = proj[:, 2 * hk:]
    layer_block = (None, bb, RET_HEADS, RET_DK, RET_DV)
    layer_spec = pl.BlockSpec(layer_block, lambda i: (layer, i, 0, 0, 0))
    stack_spec = pl.BlockSpec((state.shape[0],) + layer_block[1:], lambda i: (0, i, 0, 0, 0))
    rows = lambda width: pl.BlockSpec((None, bb, width), lambda i: (i, 0, 0))
    qk_hi = qk.astype(BF16)
    qk_lo = (qk - qk_hi.astype(F32)).astype(BF16)
    qkt = jnp.stack([qk_hi.T, qk_lo.T])
    assert n == LANES
    in_specs = [rows(2 * hk), rows(2 * hv), _resident(qkt.shape), layer_spec,
                _resident((RET_HEADS, 1, 1)), _resident((RET_HEADS, 1, 1))]
    args = [qk.reshape(n // bb, bb, 2 * hk), vg.reshape(n // bb, bb, 2 * hv), qkt, state, xi, g_c]
    aliases = {}
    if not first:
        in_specs.append(pl.BlockSpec(memory_space=pl.ANY))
        args.append(new_state)
        aliases = {len(args) - 1: 1}
    o, s = pl.pallas_call(
        functools.partial(_ret_sample_kernel, bb, layer, first),
        grid=(n // bb,),
        in_specs=in_specs,
        out_specs=[rows(hv), stack_spec if first else layer_spec],
        out_shape=[jax.ShapeDtypeStruct((n // bb, bb, hv), BF16),
                   jax.ShapeDtypeStruct(state.shape, F32)],
        input_output_aliases=aliases,
        compiler_params=_cparams("parallel"),
        name="ret_sample",
    )(*args)
    return o.reshape(n, hv), s


def _pool_prompt_kernel(tl, x_ref, halo_ref, g0_ref, w_ref, sc_ref, g1_ref, o_ref, last_ref, lvl_ref):
    assert POOL_WINDOWS == tuple(2 ** (k + 1) for k in range(len(POOL_WINDOWS)))
    i = pl.program_id(1)
    d = x_ref.shape[1]
    n = POOL_HALO + tl
    x = x_ref[...]
    xn = _rms(x, g0_ref[...])
    hn = _rms(halo_ref[...], g0_ref[...])
    lvl_ref[:, 0:POOL_PAD, :] = jnp.zeros((lvl_ref.shape[0], POOL_PAD, d), F32)
    lvl_ref[0, POOL_PAD:POOL_PAD + POOL_HALO, :] = jnp.where(i > 0, hn, 0.0)
    lvl_ref[0, POOL_PAD + POOL_HALO:POOL_PAD + n, :] = xn
    pos = i * tl + lax.broadcasted_iota(jnp.int32, (tl, 1), 0)
    ys = []
    for gi, w in enumerate(POOL_WINDOWS):
        cols = slice(gi * POOL_GW, d)
        half = w // 2
        sums = (lvl_ref[gi, POOL_PAD:POOL_PAD + n, cols]
                + lvl_ref[gi, POOL_PAD - half:POOL_PAD - half + n, cols])
        if gi + 1 < len(POOL_WINDOWS):
            lvl_ref[gi + 1, POOL_PAD:POOL_PAD + n, cols] = sums
        cur = xn[:, gi * POOL_GW:(gi + 1) * POOL_GW]
        cnt = jnp.minimum(pos + 1, w).astype(F32)
        z = sums[POOL_HALO:, :POOL_GW] / cnt - cur
        ys.append(jnp.dot(z.astype(BF16), w_ref[gi], preferred_element_type=F32))
    y = jnp.concatenate(ys, axis=1) * sc_ref[...]
    o_ref[...] = x + _rms(y, g1_ref[...])
    last_ref[...] = lvl_ref[0, POOL_PAD + tl:POOL_PAD + n, :]


def pool_prompt(x, g0, w_grp, layer, scale, g1, tl=512):
    batch, seq, d = x.shape
    hb = tl // POOL_HALO
    vec = pl.BlockSpec((1, d), lambda b, i: (0, 0))
    return pl.pallas_call(
        functools.partial(_pool_prompt_kernel, tl),
        grid=(batch, seq // tl),
        in_specs=[pl.BlockSpec((None, tl, d), lambda b, i: (b, i, 0)),
                  pl.BlockSpec((None, POOL_HALO, d), lambda b, i: (b, jnp.maximum(i * hb - 1, 0), 0)),
                  vec,
                  _layer_resident(w_grp, layer),
                  vec, vec],
        out_specs=[pl.BlockSpec((None, tl, d), lambda b, i: (b, i, 0)),
                   pl.BlockSpec((None, POOL_HALO, d), lambda b, i: (b, 0, 0))],
        out_shape=[jax.ShapeDtypeStruct(x.shape, F32),
                   jax.ShapeDtypeStruct((batch, POOL_HALO, d), F32)],
        scratch_shapes=[pltpu.VMEM((len(POOL_WINDOWS), POOL_PAD + POOL_HALO + tl, d), F32)],
        compiler_params=_cparams("parallel", "arbitrary"),
        name="pool_prompt",
    )(x, x, g0, w_grp, scale, g1)


def _pool_sample_kernel(x_ref, st_ref, g0_ref, w_ref, sc_ref, g1_ref, o_ref, nst_ref):
    d = D_MODEL
    x = x_ref[...]
    xn = _rms(x, g0_ref[...])
    ys = []
    for gi, w in enumerate(POOL_WINDOWS):
        acc = xn[:, gi * POOL_GW:(gi + 1) * POOL_GW]
        cur = acc
        for j in range(1, w):
            base = (POOL_PREV - j) * d + gi * POOL_GW
            acc = acc + st_ref[:, base:base + POOL_GW]
        cnt = float(min(PAST_LEN + 1, w))
        z = acc / cnt - cur
        ys.append(jnp.dot(z.astype(BF16), w_ref[gi], preferred_element_type=F32))
    y = jnp.concatenate(ys, axis=1) * sc_ref[...]
    o_ref[...] = x + _rms(y, g1_ref[...])
    nst_ref[:, :(POOL_PREV - 1) * d] = st_ref[:, d:]
    nst_ref[:, (POOL_PREV - 1) * d:] = xn


def pool_sample(x, state, g0, w_grp, layer, scale, g1, tb=32):
    n, d = x.shape
    vec = pl.BlockSpec((1, d), lambda i: (0, 0))
    return pl.pallas_call(
        _pool_sample_kernel,
        grid=(n // tb,),
        in_specs=[pl.BlockSpec((tb, d), lambda i: (i, 0)),
                  pl.BlockSpec((tb, POOL_PREV * d), lambda i: (i, 0)),
                  vec,
                  _layer_resident(w_grp, layer),
                  vec, vec],
        out_specs=[pl.BlockSpec((tb, d), lambda i: (i, 0)),
                   pl.BlockSpec((tb, POOL_PREV * d), lambda i: (i, 0))],
        out_shape=[jax.ShapeDtypeStruct((n, d), F32),
                   jax.ShapeDtypeStruct(state.shape, F32)],
        compiler_params=_cparams("parallel"),
        name="pool_sample",
    )(x, state, g0, w_grp, scale, g1)


def _attn_proj_kernel(dil, tn, x_ref, g_ref, w_ref, perm_ref, kv_ref, res_ref, tmp_ref):
    rows, kv_cols = kv_ref.shape
    t = x_ref.shape[0]
    nq = w_ref.shape[1] - kv_cols
    xn = _rms(x_ref[...], g_ref[...]).astype(BF16)
    nchunks = w_ref.shape[1] // tn
    lane_tiles = tn // LANES

    d1 = min(dil, ATT_REGROUP_RADIX)
    d2 = dil // d1

    def regroup(j):
        for c in range(lane_tiles):
            slot = (j * lane_tiles + c) % res_ref.shape[0]
            lanes = slice(j * tn + c * LANES, j * tn + (c + 1) * LANES)
            if d2 == 1:
                for r in range(dil):
                    perm_ref[r, :, lanes] = res_ref[slot, pl.ds(r, t // dil, stride=dil), :].astype(BF16)
                continue
            for r1 in range(d1):
                tmp_ref[slot, r1 * (t // d1):(r1 + 1) * (t // d1), :] = res_ref[slot, pl.ds(r1, t // d1, stride=d1), :]
            for r1 in range(d1):
                for r2 in range(d2):
                    perm_ref[r2 * d1 + r1, :, lanes] = (
                        tmp_ref[slot, pl.ds(r1 * (t // d1) + r2, t // dil, stride=d2), :].astype(BF16))

    for j in range(nchunks):
        cols = slice(j * tn, (j + 1) * tn)
        res = jnp.dot(xn, w_ref[:, cols], preferred_element_type=F32)
        if dil == 1:
            perm_ref[0, :, cols] = res.astype(BF16)
        else:
            for c in range(lane_tiles):
                res_ref[(j * lane_tiles + c) % res_ref.shape[0]] = res[:, c * LANES:(c + 1) * LANES]
        if j * tn >= nq:
            kv_ref[:, j * tn - nq:(j + 1) * tn - nq] = res[t - rows:, :]
        if dil > 1 and j > 0:
            regroup(j - 1)
    if dil > 1:
        regroup(nchunks - 1)


def attn_proj(x, g, w, layer, gi, t=1024, tn=256):
    batch, seq, d = x.shape
    win, dil = ATT_GROUPS[gi]
    kv_rows = min(win, seq)
    nq = ATT_HEADS * ATT_DH
    tkv = min(t, kv_rows)
    first = (seq - kv_rows) // t
    kv_index = lambda b, i: (b, jnp.maximum(i - first, 0), 0)
    perm, kv = pl.pallas_call(
        functools.partial(_attn_proj_kernel, dil, tn),
        grid=(batch, seq // t),
        in_specs=[pl.BlockSpec((None, t, d), lambda b, i: (b, i, 0)),
                  _resident((1, d)),
                  pl.BlockSpec((None, d, ATT_GW), lambda b, i: (layer, 0, gi), pipeline_mode=pl.Buffered(1))],
        out_specs=[pl.BlockSpec((None, dil, t // dil, ATT_GW), lambda b, i: (b, 0, i, 0)),
                   pl.BlockSpec((None, tkv, ATT_GW - nq), kv_index)],
        out_shape=[jax.ShapeDtypeStruct((batch, dil, seq // dil, ATT_GW), BF16),
                   jax.ShapeDtypeStruct((batch, kv_rows, ATT_GW - nq), F32)],
        scratch_shapes=[pltpu.VMEM((2 * tn // LANES, t, LANES), F32)] * 2,
        compiler_params=_cparams("parallel", "arbitrary"),
        name="attn_proj_g%d" % gi,
    )(x, g, w)
    return perm.reshape(batch, seq, ATT_GW), kv


def _alibi_slope(group, head):
    n = len(ATT_GROUPS) * ATT_HEADS
    idx = jnp.full((ATT_BLK, ATT_BLK), group * ATT_HEADS + 1, F32) + head
    return jnp.exp2(-8.0 * idx / n)


def _attn_prompt_kernel(seq, *refs):
    ng = len(ATT_GROUPS)
    qkv = [refs[3 * g:3 * g + 3] for g in range(ng)]
    out_ref, o_s, m_s, l_s = refs[3 * ng:]
    hp = pl.program_id(1)
    lane = lax.broadcasted_iota(jnp.int32, (ATT_BLK, LANES), 1)
    first_head = lane < ATT_DH
    qi = lax.broadcasted_iota(jnp.int32, (ATT_BLK, ATT_BLK), 0)
    kj = lax.broadcasted_iota(jnp.int32, (ATT_BLK, ATT_BLK), 1)
    nt = (((1,), (1,)), ((), ()))
    ones = jnp.ones((2 * ATT_BLK, LANES), BF16)

    for g, (win, dil) in enumerate(ATT_GROUPS):
        q_ref, k_ref, v_ref = qkv[g]
        n_keys = win // dil
        ls = seq // dil
        nb = ls // ATT_BLK
        d_cur = qi - kj
        d_prev = qi + ATT_BLK - kj
        bias_cur, bias_prev = [], []
        for hh in range(2):
            slope = _alibi_slope(g, (hp * 2 + hh).astype(F32)) * LOG2E
            bias_cur.append(jnp.where(d_cur >= 0, -slope * (dil * d_cur).astype(F32), NEG_INF))
            bias_prev.append(jnp.where(d_prev <= n_keys, -slope * (dil * d_prev).astype(F32), NEG_INF))
        bias_cur = jnp.concatenate(bias_cur, axis=0)
        bias_prev = jnp.concatenate(bias_prev, axis=0)
        rows = lambda qb: slice(qb * ATT_BLK, (qb + 1) * ATT_BLK)

        bias_both = jnp.concatenate([bias_prev, bias_cur], axis=1)

        def keys(qb):
            return slice((qb - 1) * ATT_BLK, (qb + 1) * ATT_BLK) if qb % nb > 0 else rows(qb)

        for c0 in range(0, seq // ATT_BLK, ATT_INTERLEAVE):
            blocks = range(c0, c0 + ATT_INTERLEAVE)
            qm = {}
            for qb in blocks:
                qf = q_ref[rows(qb), :].astype(F32)
                qm[qb] = jnp.concatenate([jnp.where(first_head, qf, 0.0), jnp.where(first_head, 0.0, qf)],
                                         axis=0).astype(BF16)
            s = {qb: lax.dot_general(qm[qb], k_ref[keys(qb), :], nt, preferred_element_type=F32)
                 + (bias_both if qb % nb > 0 else bias_cur) for qb in blocks}
            halves = (slice(0, ATT_BLK), slice(ATT_BLK, 2 * ATT_BLK))
            m = {(qb, hh): jnp.max(s[qb][half], axis=-1, keepdims=True)
                 for qb in blocks for hh, half in enumerate(halves)}
            p = {qb: jnp.concatenate([jnp.exp2(s[qb][half] - m[qb, hh]) for hh, half in enumerate(halves)],
                                     axis=0).astype(BF16) for qb in blocks}
            for qb in blocks:
                v = v_ref[keys(qb), :]
                acc = jnp.dot(p[qb], jnp.concatenate([v, ones[:v.shape[0]]], axis=1), preferred_element_type=F32)
                blk, res = qb % nb, qb // nb
                dst = pl.ds(blk * ATT_BLK * dil + res, ATT_BLK, stride=dil) if dil > 1 else rows(qb)
                o_s[g, dst, :] = jnp.where(first_head, acc[:ATT_BLK, :LANES], acc[ATT_BLK:, :LANES])
                l_s[g, dst, :] = jnp.where(first_head, acc[:ATT_BLK, LANES:], acc[ATT_BLK:, LANES:])
                m_s[g, dst, :] = jnp.where(first_head, m[qb, 0], m[qb, 1])

    m_all = m_s[0]
    for g in range(1, ng):
        m_all = jnp.maximum(m_all, m_s[g])
    num = jnp.zeros((seq, LANES), F32)
    den = jnp.zeros((seq, LANES), F32)
    for g in range(ng):
        e = jnp.exp2(m_s[g] - m_all)
        num = num + e * o_s[g]
        den = den + e * l_s[g]
    out_ref[...] = (num / den).astype(out_ref.dtype)


def attn_prompt(qkv_groups, batch, seq):
    ng = len(ATT_GROUPS)
    hpairs = ATT_HEADS * ATT_DH // LANES
    in_specs, args = [], []
    for g in range(ng):
        for part in range(3):
            in_specs.append(pl.BlockSpec((seq, LANES), lambda b, h, part=part: (b, part * hpairs + h)))
            args.append(qkv_groups[g])
    return pl.pallas_call(
        functools.partial(_attn_prompt_kernel, seq),
        grid=(batch, hpairs),
        in_specs=in_specs,
        out_specs=pl.BlockSpec((seq, LANES), lambda b, h: (b, h)),
        out_shape=jax.ShapeDtypeStruct((batch * seq, D_MODEL), BF16),
        scratch_shapes=[pltpu.VMEM((ng, seq, LANES), F32)] * 3,
        compiler_params=_cparams("parallel", "parallel"),
        name="attn_prompt",
    )(*args)


def _attn_sample_kernel(hb, qkv_ref, slope_ref, c0_ref, c1_ref, c2_ref, o_ref):
    hg = pl.program_id(0)
    b = pl.program_id(1)
    caches = (c0_ref, c1_ref, c2_ref)
    ng = len(ATT_GROUPS)

    @pl.when(b == 0)
    def _():
        o_ref[...] = jnp.zeros_like(o_ref)

    this_seq = lax.broadcasted_iota(jnp.int32, (ATT_DH, LANES), 1) == b
    pick = jnp.where(lax.broadcasted_iota(jnp.int32, (LANES, LANES), 0) == b, 1.0, 0.0).astype(BF16)
    lane_b = [[jnp.dot(qkv_ref[0, g, part], pick, preferred_element_type=F32)
               + jnp.dot(qkv_ref[1, g, part], pick, preferred_element_type=F32)
               for part in range(3)] for g in range(ng)]

    for hh in range(hb):
        rows = slice(hh * ATT_DH, (hh + 1) * ATT_DH)
        cols = [[lane_b[g][part][rows] for part in range(3)] for g in range(ng)]
        scores, new_scores = [], []
        m = None
        for g, (_, dil) in enumerate(ATT_GROUPS):
            wb = caches[g].shape[-1]
            qc = cols[g][0]
            s = jnp.sum(caches[g][0, hh] * jnp.concatenate([qc] * (wb // LANES), axis=1),
                        axis=0, keepdims=True)
            pos = lax.broadcasted_iota(jnp.int32, (1, wb), 1)
            slope = slope_ref[g, hg * hb + hh] * LOG2E
            s = jnp.where((pos & (dil - 1)) == 0, s - slope * (wb - pos).astype(F32), NEG_INF)
            s_new = jnp.sum(qc * cols[g][1], axis=0, keepdims=True)[:, :1]
            mg = jnp.maximum(jnp.max(s, axis=1, keepdims=True), s_new)
            m = mg if m is None else jnp.maximum(m, mg)
            scores.append(s)
            new_scores.append(s_new)
        l = jnp.zeros((1, 1), F32)
        o = jnp.zeros((ATT_DH, 1), F32)
        for g in range(ng):
            p = jnp.exp2(scores[g] - m)
            p_new = jnp.exp2(new_scores[g] - m)
            l = l + jnp.sum(p, axis=1, keepdims=True) + p_new
            o = o + jnp.sum(caches[g][1, hh] * p, axis=1, keepdims=True) + p_new * cols[g][2]
        o_ref[rows, :] = jnp.where(this_seq, o / l, o_ref[rows, :])


def attn_sample(proj, caches, layer, hb=8):
    n = proj.shape[0]
    ng = len(ATT_GROUPS)
    hd = ATT_HEADS * ATT_DH
    assert n == LANES
    for c, (win, dil) in zip(caches, ATT_GROUPS):
        assert c.shape[2] == win and win % dil == 0 and dil & (dil - 1) == 0
    nall = ng * ATT_HEADS
    slopes = jnp.exp2(-8.0 * jnp.arange(1, nall + 1, dtype=F32) / nall).reshape(ng, ATT_HEADS)
    proj_hi = proj.astype(BF16)
    proj_lo = (proj - proj_hi.astype(F32)).astype(BF16)
    qkv_t = jnp.stack([proj_hi.T, proj_lo.T]).reshape(2, ng, 3, hd, n)
    cache_t = [jnp.transpose(c, (0, 1, 3, 4, 5, 2)) for c in caches]
    cache_specs = [pl.BlockSpec((None, None, 2, hb, ATT_DH, c.shape[-1]), lambda h, b: (layer, b, 0, h, 0, 0))
                   for c in cache_t]
    out_t = pl.pallas_call(
        functools.partial(_attn_sample_kernel, hb),
        grid=(ATT_HEADS // hb, n),
        in_specs=[pl.BlockSpec((2, ng, 3, hb * ATT_DH, n), lambda h, b: (0, 0, 0, h, 0)),
                  pl.BlockSpec(memory_space=pltpu.SMEM)] + cache_specs,
        out_specs=pl.BlockSpec((hb * ATT_DH, n), lambda h, b: (h, 0)),
        out_shape=jax.ShapeDtypeStruct((hd, n), F32),
        compiler_params=_cparams("parallel", "arbitrary"),
        name="attn_sample",
    )(qkv_t, slopes, *cache_t)
    return out_t.T


def kernel(x_prompt, x_sample, state_ret, state_pool, cache_kv_g0, cache_kv_g1, cache_kv_g2, norm_gains,
           w_in_ret, w_out_ret, w_grp_pool, scale_pool, w_in_attn, w_out_attn, w_ffn_in, w_ffn_out):
    batch, seq, d = x_prompt.shape
    ns = x_sample.shape[0]
    caches = (cache_kv_g0, cache_kv_g1, cache_kv_g2)
    ng = len(ATT_GROUPS)
    xp = x_prompt.reshape(batch * seq, d)
    xs = x_sample.reshape(ns, d)
    ret_p, pool_p, pool_s = [], [], []
    ret_s = None
    kv_p = [[] for _ in range(ng)]
    kv_s = [[] for _ in range(ng)]

    col = jnp.arange(w_in_attn.shape[-1]) % ATT_GW
    w_in_attn = w_in_attn * jnp.where(col < ATT_HEADS * ATT_DH, ATT_Q_SCALE, 1.0).astype(F32)
    w_in_ret, w_out_ret, w_grp_pool, w_in_attn, w_out_attn, w_ffn_in, w_ffn_out = (
        w.astype(BF16) for w in (w_in_ret, w_out_ret, w_grp_pool, w_in_attn, w_out_attn, w_ffn_in, w_ffn_out))

    for i in range(DEPTH):
        kind, j = i % N_MIXERS, i // N_MIXERS
        g = [norm_gains[i, r][None, :] for r in range(4)]
        mix_p = mix_s = None
        if kind == 0:
            proj = norm_matmul(xp, g[0], w_in_ret, j, BF16, 1024)
            o, sp = ret_prompt(proj, batch, seq)
            mix_p = (o, w_out_ret, j, g[1])
            proj_s = norm_matmul(xs, g[0], w_in_ret, j, F32, ns)
            o_s, ret_s = ret_sample(proj_s, state_ret, j, ret_s)
            mix_s = (o_s, w_out_ret, j, g[1])
            ret_p.append(sp)
        elif kind == 1:
            scale = scale_pool[j][None, :]
            xp3, last = pool_prompt(xp.reshape(batch, seq, d), g[0], w_grp_pool, j, scale, g[1])
            xp = xp3.reshape(batch * seq, d)
            xs, nst = pool_sample(xs, state_pool[j].reshape(ns, POOL_PREV * d), g[0], w_grp_pool, j, scale, g[1])
            pool_p.append(last[:, POOL_HALO - POOL_PREV:])
            pool_s.append(nst.reshape(ns, POOL_PREV, d))
        else:
            x3 = xp.reshape(batch, seq, d)
            qkv = []
            for gi in range(ng):
                perm, kv = attn_proj(x3, g[0], w_in_attn, j, gi)
                qkv.append(perm.reshape(batch * seq, ATT_GW))
                kv_p[gi].append(kv.reshape(batch, kv.shape[1], 2, ATT_HEADS, ATT_DH))
            mix_p = (attn_prompt(qkv, batch, seq), w_out_attn, j, g[1])
            proj_s = norm_matmul(xs, g[0], w_in_attn, j, F32, ns)
            mix_s = (attn_sample(proj_s, caches, j).astype(BF16), w_out_attn, j, g[1])
            p5 = proj_s.reshape(ns, 1, ng, 3, ATT_HEADS, ATT_DH)
            for gi in range(ng):
                kv_s[gi].append(p5[:, :, gi, 1:])
        xp = ffn(xp, g[2], w_ffn_in, w_ffn_out, i, g[3], 1024 if mix_p is None else 512, mixer=mix_p)
        xs = ffn(xs, g[2], w_ffn_in, w_ffn_out, i, g[3], ns, mixer=mix_s)

    return (xp.reshape(batch, seq, d), xs.reshape(ns, 1, d),
            jnp.stack(ret_p), ret_s,
            jnp.stack(pool_p), jnp.stack(pool_s),
            jnp.stack(kv_p[0]), jnp.stack(kv_s[0]),
            jnp.stack(kv_p[1]), jnp.stack(kv_s[1]),
            jnp.stack(kv_p[2]), jnp.stack(kv_s[2]))
```

```python
import functools

import jax
import jax.numpy as jnp
from jax import lax
from jax.experimental import pallas as pl
from jax.experimental.pallas import tpu as pltpu

F32 = jnp.float32
BF16 = jnp.bfloat16

D_MODEL = 1024
DEPTH = 4
PAST_LEN = 2048
N_MIXERS = 3

RET_HEADS = 4
RET_DK = D_MODEL // RET_HEADS
RET_DV = 2 * D_MODEL // RET_HEADS
RET_PROMPT_CHUNK = 256

POOL_WINDOWS = (2, 4, 8, 16)
POOL_GW = D_MODEL // len(POOL_WINDOWS)
POOL_PREV = max(POOL_WINDOWS) - 1
POOL_HALO = POOL_PREV + 1
POOL_PAD = 8

ATT_GROUPS = ((128, 1), (512, 4), (2048, 16))
ATT_HEADS = 16
ATT_DH = D_MODEL // ATT_HEADS
ATT_GW = 3 * ATT_HEADS * ATT_DH
ATT_BLK = 128
ATT_REGROUP_RADIX = 4
ATT_INTERLEAVE = 8

FFN_HIDDEN = 2816

NORM_EPS = 1e-6
GN_EPS = 1e-5
NEG_INF = -1e30
LOG2E = 1.4426950408889634
ATT_Q_SCALE = ATT_DH ** -0.5 * LOG2E

LANES = 128
VMEM_LIMIT = 56 * 1024 * 1024


def _cparams(*sem):
    return pltpu.CompilerParams(dimension_semantics=sem, vmem_limit_bytes=VMEM_LIMIT)


def _rms(x, g):
    return x * lax.rsqrt(jnp.mean(x * x, axis=-1, keepdims=True) + NORM_EPS) * g


def _silu(x):
    return x * jax.nn.sigmoid(x)


def _resident(shape):
    return pl.BlockSpec(shape, lambda *_: (0,) * len(shape), pipeline_mode=pl.Buffered(1))


def _layer_resident(w, layer):
    return pl.BlockSpec((None,) + w.shape[1:], lambda *_: (layer,) + (0,) * (w.ndim - 1),
                        pipeline_mode=pl.Buffered(1))


def _norm_matmul_kernel(tn, x_ref, g_ref, w_ref, o_ref):
    xn = _rms(x_ref[...], g_ref[...]).astype(BF16)
    for j in range(w_ref.shape[1] // tn):
        cols = slice(j * tn, (j + 1) * tn)
        o_ref[:, cols] = jnp.dot(xn, w_ref[:, cols], preferred_element_type=F32).astype(o_ref.dtype)


def norm_matmul(x, g, w, layer, out_dtype, tm, tn=512):
    m, d = x.shape
    n = w.shape[2]
    return pl.pallas_call(
        functools.partial(_norm_matmul_kernel, tn),
        grid=(m // tm,),
        in_specs=[pl.BlockSpec((tm, d), lambda i: (i, 0)), _resident((1, d)), _layer_resident(w, layer)],
        out_specs=pl.BlockSpec((tm, n), lambda i: (i, 0)),
        out_shape=jax.ShapeDtypeStruct((m, n), out_dtype),
        compiler_params=_cparams("parallel"),
        name="norm_matmul",
    )(x, g, w)


def _ffn_kernel(th, mixer, x_ref, g2_ref, wi_ref, wo_ref, g3_ref, *rest):
    o_ref, a_ref = rest[-2:]
    x = x_ref[...]
    if mixer:
        m_ref, wm_ref, g1_ref = rest[:3]
        x = x + _rms(jnp.dot(m_ref[...], wm_ref[...], preferred_element_type=F32), g1_ref[...])
    xn = _rms(x, g2_ref[...]).astype(BF16)
    for j in range(FFN_HIDDEN // th):
        gate = jnp.dot(xn, wi_ref[:, j * th:(j + 1) * th], preferred_element_type=F32)
        up = jnp.dot(xn, wi_ref[:, FFN_HIDDEN + j * th:FFN_HIDDEN + (j + 1) * th], preferred_element_type=F32)
        a_ref[:, j * th:(j + 1) * th] = (_silu(gate) * up).astype(BF16)
    y = jnp.dot(a_ref[...], wo_ref[...], preferred_element_type=F32)
    o_ref[...] = x + _rms(y, g3_ref[...])


def ffn(x, g2, w_in, w_out, layer, g3, tm, th=256, mixer=None):
    m, d = x.shape
    in_specs = [pl.BlockSpec((tm, d), lambda i: (i, 0)),
                _resident((1, d)), _layer_resident(w_in, layer), _layer_resident(w_out, layer),
                _resident((1, d))]
    args = [x, g2, w_in, w_out, g3]
    if mixer is not None:
        mix, w_mix, mix_layer, g1 = mixer
        in_specs += [pl.BlockSpec((tm, mix.shape[1]), lambda i: (i, 0)),
                     _layer_resident(w_mix, mix_layer), _resident((1, d))]
        args += [mix, w_mix, g1]
    return pl.pallas_call(
        functools.partial(_ffn_kernel, th, mixer is not None),
        grid=(m // tm,),
        in_specs=in_specs,
        out_specs=pl.BlockSpec((tm, d), lambda i: (i, 0)),
        out_shape=jax.ShapeDtypeStruct((m, d), F32),
        scratch_shapes=[pltpu.VMEM((tm, FFN_HIDDEN), BF16)],
        compiler_params=_cparams("parallel"),
        name="ffn",
    )(*args)


def _ret_tables(c):
    lg = jnp.log1p(-jnp.exp2(-5.0 - jnp.arange(RET_HEADS, dtype=F32)))
    n = jnp.arange(c, dtype=F32)
    diff = n[:, None] - n[None, :]
    decay = jnp.where(diff[None] >= 0, jnp.exp(jnp.maximum(diff, 0.0)[None] * lg[:, None, None]), 0.0)
    xi = jnp.exp((n[:, None] + 1.0) * lg[None, :])
    zeta = jnp.exp((c - 1.0 - n)[:, None] * lg[None, :])
    g_c = jnp.exp(c * lg)
    return decay, xi.T[:, :, None], zeta.T[:, :, None], g_c[:, None, None]


def _group_norm_gate(o, gate):
    mu = jnp.mean(o, axis=-1, keepdims=True)
    oc = o - mu
    var = jnp.mean(oc * oc, axis=-1, keepdims=True)
    return oc * lax.rsqrt(var + GN_EPS) * _silu(gate)


def _ret_prompt_kernel(nsub, c, p_ref, decay_ref, xi_ref, zeta_ref, gc_ref, o_ref, s_ref):
    @pl.when(pl.program_id(1) == 0)
    def _():
        s_ref[...] = jnp.zeros_like(s_ref)

    hk = RET_HEADS * RET_DK
    hv = RET_HEADS * RET_DV
    nt = (((1,), (1,)), ((), ()))
    for h in range(RET_HEADS):
        s = s_ref[h]
        for cc in range(nsub):
            rows = slice(cc * c, (cc + 1) * c)
            q = p_ref[rows, h * RET_DK:(h + 1) * RET_DK] * (RET_DK ** -0.5)
            k = p_ref[rows, hk + h * RET_DK:hk + (h + 1) * RET_DK]
            v = p_ref[rows, 2 * hk + h * RET_DV:2 * hk + (h + 1) * RET_DV]
            gt = p_ref[rows, 2 * hk + hv + h * RET_DV:2 * hk + hv + (h + 1) * RET_DV]
            a = lax.dot_general(q, k, nt, preferred_element_type=F32) * decay_ref[h]
            o = (jnp.dot(a.astype(BF16), v, preferred_element_type=F32)
                 + jnp.dot(q, s.astype(BF16), preferred_element_type=F32) * xi_ref[h])
            kz = (k.astype(F32) * zeta_ref[h]).T.astype(BF16)
            s = gc_ref[h] * s + jnp.dot(kz, v, preferred_element_type=F32)
            o_ref[rows, h * RET_DV:(h + 1) * RET_DV] = _group_norm_gate(o, gt.astype(F32)).astype(o_ref.dtype)
        s_ref[h] = s


def ret_prompt(proj, batch, seq, chunk=RET_PROMPT_CHUNK, nsub=2):
    rows = nsub * chunk
    nstep = seq // rows
    decay, xi, zeta, g_c = _ret_tables(chunk)
    full = lambda a: pl.BlockSpec(a.shape, lambda b, i: (0,) * a.ndim)
    return pl.pallas_call(
        functools.partial(_ret_prompt_kernel, nsub, chunk),
        grid=(batch, nstep),
        in_specs=[pl.BlockSpec((rows, proj.shape[1]), lambda b, i: (b * nstep + i, 0)),
                  full(decay), full(xi), full(zeta), full(g_c)],
        out_specs=[pl.BlockSpec((rows, RET_HEADS * RET_DV), lambda b, i: (b * nstep + i, 0)),
                   pl.BlockSpec((None, RET_HEADS, RET_DK, RET_DV), lambda b, i: (b, 0, 0, 0))],
        out_shape=[jax.ShapeDtypeStruct((batch * seq, RET_HEADS * RET_DV), BF16),
                   jax.ShapeDtypeStruct((batch, RET_HEADS, RET_DK, RET_DV), F32)],
        compiler_params=_cparams("parallel", "arbitrary"),
        name="ret_prompt",
    )(proj, decay, xi, zeta, g_c)


def _ret_sample_kernel(bb, layer, first, qk_ref, vg_ref, qkt_ref, s0_ref, xi_ref, gc_ref, *rest):
    o_ref, s_ref = rest[-2:]
    hk = RET_HEADS * RET_DK
    hv = RET_HEADS * RET_DV
    n = qkt_ref.shape[2]
    scale = RET_DK ** -0.5
    seq_id = lax.broadcasted_iota(jnp.int32, (n, LANES), 0)
    widen = lambda col: jnp.concatenate([col] * (RET_DV // LANES), axis=1)
    for i in range(bb):
        pick = jnp.where(seq_id == pl.program_id(0) * bb + i, 1.0, 0.0).astype(BF16)
        qk_col = (jnp.dot(qkt_ref[0], pick, preferred_element_type=F32)
                  + jnp.dot(qkt_ref[1], pick, preferred_element_type=F32))
        for h in range(RET_HEADS):
            q_row = qk_ref[i:i + 1, h * RET_DK:(h + 1) * RET_DK] * scale
            k_row = qk_ref[i:i + 1, hk + h * RET_DK:hk + (h + 1) * RET_DK]
            v_row = vg_ref[i:i + 1, h * RET_DV:(h + 1) * RET_DV]
            g_row = vg_ref[i:i + 1, hv + h * RET_DV:hv + (h + 1) * RET_DV]
            q_col = widen(qk_col[h * RET_DK:(h + 1) * RET_DK] * scale)
            k_col = widen(qk_col[hk + h * RET_DK:hk + (h + 1) * RET_DK])
            s0 = s0_ref[i, h]
            a = jnp.sum(q_row * k_row, axis=-1, keepdims=True)
            qs = jnp.sum(q_col * s0, axis=0, keepdims=True)
            o = a * v_row + qs * xi_ref[h]
            s_new = gc_ref[h] * s0 + k_col * v_row
            if first:
                s_ref[layer, i, h] = s_new
            else:
                s_ref[i, h] = s_new
            o_ref[i:i + 1, h * RET_DV:(h + 1) * RET_DV] = _group_norm_gate(o, g_row).astype(o_ref.dtype)
    if first:
        for other in range(s_ref.shape[0]):
            if other != layer:
                s_ref[other] = jnp.zeros(s_ref.shape[1:], F32)


def ret_sample(proj, state, layer, new_state=None, bb=2):
    n = proj.shape[0]
    hk = RET_HEADS * RET_DK
    hv = RET_HEADS * RET_DV
    first = new_state is None
    _, xi, _, g_c = _ret_tables(1)
    qk = proj[:, :2 * hk]
    vg = proj[:, 2 * hk:]
    layer_block = (None, bb, RET_HEADS, RET_DK, RET_DV)
    layer_spec = pl.BlockSpec(layer_block, lambda i: (layer, i, 0, 0, 0))
    stack_spec = pl.BlockSpec((state.shape[0],) + layer_block[1:], lambda i: (0, i, 0, 0, 0))
    rows = lambda width: pl.BlockSpec((None, bb, width), lambda i: (i, 0, 0))
    qk_hi = qk.astype(BF16)
    qk_lo = (qk - qk_hi.astype(F32)).astype(BF16)
    qkt = jnp.stack([qk_hi.T, qk_lo.T])
    assert n == LANES
    in_specs = [rows(2 * hk), rows(2 * hv), _resident(qkt.shape), layer_spec,
                _resident((RET_HEADS, 1, 1)), _resident((RET_HEADS, 1, 1))]
    args = [qk.reshape(n // bb, bb, 2 * hk), vg.reshape(n // bb, bb, 2 * hv), qkt, state, xi, g_c]
    aliases = {}
    if not first:
        in_specs.append(pl.BlockSpec(memory_space=pl.ANY))
        args.append(new_state)
        aliases = {len(args) - 1: 1}
    o, s = pl.pallas_call(
        functools.partial(_ret_sample_kernel, bb, layer, first),
        grid=(n // bb,),
        in_specs=in_specs,
        out_specs=[rows(hv), stack_spec if first else layer_spec],
        out_shape=[jax.ShapeDtypeStruct((n // bb, bb, hv), BF16),
                   jax.ShapeDtypeStruct(state.shape, F32)],
        input_output_aliases=aliases,
        compiler_params=_cparams("parallel"),
        name="ret_sample",
    )(*args)
    return o.reshape(n, hv), s


def _pool_prompt_kernel(tl, x_ref, halo_ref, g0_ref, w_ref, sc_ref, g1_ref, o_ref, last_ref, lvl_ref):
    assert POOL_WINDOWS == tuple(2 ** (k + 1) for k in range(len(POOL_WINDOWS)))
    i = pl.program_id(1)
    d = x_ref.shape[1]
    n = POOL_HALO + tl
    x = x_ref[...]
    xn = _rms(x, g0_ref[...])
    hn = _rms(halo_ref[...], g0_ref[...])
    lvl_ref[:, 0:POOL_PAD, :] = jnp.zeros((lvl_ref.shape[0], POOL_PAD, d), F32)
    lvl_ref[0, POOL_PAD:POOL_PAD + POOL_HALO, :] = jnp.where(i > 0, hn, 0.0)
    lvl_ref[0, POOL_PAD + POOL_HALO:POOL_PAD + n, :] = xn
    pos = i * tl + lax.broadcasted_iota(jnp.int32, (tl, 1), 0)
    ys = []
    for gi, w in enumerate(POOL_WINDOWS):
        cols = slice(gi * POOL_GW, d)
        half = w // 2
        sums = (lvl_ref[gi, POOL_PAD:POOL_PAD + n, cols]
                + lvl_ref[gi, POOL_PAD - half:POOL_PAD - half + n, cols])
        if gi + 1 < len(POOL_WINDOWS):
            lvl_ref[gi + 1, POOL_PAD:POOL_PAD + n, cols] = sums
        cur = xn[:, gi * POOL_GW:(gi + 1) * POOL_GW]
        cnt = jnp.minimum(pos + 1, w).astype(F32)
        z = sums[POOL_HALO:, :POOL_GW] / cnt - cur
        ys.append(jnp.dot(z.astype(BF16), w_ref[gi], preferred_element_type=F32))
    y = jnp.concatenate(ys, axis=1) * sc_ref[...]
    o_ref[...] = x + _rms(y, g1_ref[...])
    last_ref[...] = lvl_ref[0, POOL_PAD + tl:POOL_PAD + n, :]


def pool_prompt(x, g0, w_grp, layer, scale, g1, tl=512):
    batch, seq, d = x.shape
    hb = tl // POOL_HALO
    vec = pl.BlockSpec((1, d), lambda b, i: (0, 0))
    return pl.pallas_call(
        functools.partial(_pool_prompt_kernel, tl),
        grid=(batch, seq // tl),
        in_specs=[pl.BlockSpec((None, tl, d), lambda b, i: (b, i, 0)),
                  pl.BlockSpec((None, POOL_HALO, d), lambda b, i: (b, jnp.maximum(i * hb - 1, 0), 0)),
                  vec,
                  _layer_resident(w_grp, layer),
                  vec, vec],
        out_specs=[pl.BlockSpec((None, tl, d), lambda b, i: (b, i, 0)),
                   pl.BlockSpec((None, POOL_HALO, d), lambda b, i: (b, 0, 0))],
        out_shape=[jax.ShapeDtypeStruct(x.shape, F32),
                   jax.ShapeDtypeStruct((batch, POOL_HALO, d), F32)],
        scratch_shapes=[pltpu.VMEM((len(POOL_WINDOWS), POOL_PAD + POOL_HALO + tl, d), F32)],
        compiler_params=_cparams("parallel", "arbitrary"),
        name="pool_prompt",
    )(x, x, g0, w_grp, scale, g1)


def _pool_sample_kernel(x_ref, st_ref, g0_ref, w_ref, sc_ref, g1_ref, o_ref, nst_ref):
    d = D_MODEL
    x = x_ref[...]
    xn = _rms(x, g0_ref[...])
    ys = []
    for gi, w in enumerate(POOL_WINDOWS):
        acc = xn[:, gi * POOL_GW:(gi + 1) * POOL_GW]
        cur = acc
        for j in range(1, w):
            base = (POOL_PREV - j) * d + gi * POOL_GW
            acc = acc + st_ref[:, base:base + POOL_GW]
        cnt = float(min(PAST_LEN + 1, w))
        z = acc / cnt - cur
        ys.append(jnp.dot(z.astype(BF16), w_ref[gi], preferred_element_type=F32))
    y = jnp.concatenate(ys, axis=1) * sc_ref[...]
    o_ref[...] = x + _rms(y, g1_ref[...])
    nst_ref[:, :(POOL_PREV - 1) * d] = st_ref[:, d:]
    nst_ref[:, (POOL_PREV - 1) * d:] = xn


def pool_sample(x, state, g0, w_grp, layer, scale, g1, tb=32):
    n, d = x.shape
    vec = pl.BlockSpec((1, d), lambda i: (0, 0))
    return pl.pallas_call(
        _pool_sample_kernel,
        grid=(n // tb,),
        in_specs=[pl.BlockSpec((tb, d), lambda i: (i, 0)),
                  pl.BlockSpec((tb, POOL_PREV * d), lambda i: (i, 0)),
                  vec,
                  _layer_resident(w_grp, layer),
                  vec, vec],
        out_specs=[pl.BlockSpec((tb, d), lambda i: (i, 0)),
                   pl.BlockSpec((tb, POOL_PREV * d), lambda i: (i, 0))],
        out_shape=[jax.ShapeDtypeStruct((n, d), F32),
                   jax.ShapeDtypeStruct(state.shape, F32)],
        compiler_params=_cparams("parallel"),
        name="pool_sample",
    )(x, state, g0, w_grp, scale, g1)


def _attn_proj_kernel(dil, tn, x_ref, g_ref, w_ref, perm_ref, kv_ref, res_ref, tmp_ref):
    rows, kv_cols = kv_ref.shape
    t = x_ref.shape[0]
    nq = w_ref.shape[1] - kv_cols
    xn = _rms(x_ref[...], g_ref[...]).astype(BF16)
    nchunks = w_ref.shape[1] // tn
    lane_tiles = tn // LANES

    d1 = min(dil, ATT_REGROUP_RADIX)
    d2 = dil // d1

    def regroup(j):
        for c in range(lane_tiles):
            slot = (j * lane_tiles + c) % res_ref.shape[0]
            lanes = slice(j * tn + c * LANES, j * tn + (c + 1) * LANES)
            if d2 == 1:
                for r in range(dil):
                    perm_ref[r, :, lanes] = res_ref[slot, pl.ds(r, t // dil, stride=dil), :].astype(BF16)
                continue
            for r1 in range(d1):
                tmp_ref[slot, r1 * (t // d1):(r1 + 1) * (t // d1), :] = res_ref[slot, pl.ds(r1, t // d1, stride=d1), :]
            for r1 in range(d1):
                for r2 in range(d2):
                    perm_ref[r2 * d1 + r1, :, lanes] = (
                        tmp_ref[slot, pl.ds(r1 * (t // d1) + r2, t // dil, stride=d2), :].astype(BF16))

    for j in range(nchunks):
        cols = slice(j * tn, (j + 1) * tn)
        res = jnp.dot(xn, w_ref[:, cols], preferred_element_type=F32)
        if dil == 1:
            perm_ref[0, :, cols] = res.astype(BF16)
        else:
            for c in range(lane_tiles):
                res_ref[(j * lane_tiles + c) % res_ref.shape[0]] = res[:, c * LANES:(c + 1) * LANES]
        if j * tn >= nq:
            kv_ref[:, j * tn - nq:(j + 1) * tn - nq] = res[t - rows:, :]
        if dil > 1 and j > 0:
            regroup(j - 1)
    if dil > 1:
        regroup(nchunks - 1)


def attn_proj(x, g, w, layer, gi, t=1024, tn=256):
    batch, seq, d = x.shape
    win, dil = ATT_GROUPS[gi]
    kv_rows = min(win, seq)
    nq = ATT_HEADS * ATT_DH
    tkv = min(t, kv_rows)
    first = (seq - kv_rows) // t
    kv_index = lambda b, i: (b, jnp.maximum(i - first, 0), 0)
    perm, kv = pl.pallas_call(
        functools.partial(_attn_proj_kernel, dil, tn),
        grid=(batch, seq // t),
        in_specs=[pl.BlockSpec((None, t, d), lambda b, i: (b, i, 0)),
                  _resident((1, d)),
                  pl.BlockSpec((None, d, ATT_GW), lambda b, i: (layer, 0, gi), pipeline_mode=pl.Buffered(1))],
        out_specs=[pl.BlockSpec((None, dil, t // dil, ATT_GW), lambda b, i: (b, 0, i, 0)),
                   pl.BlockSpec((None, tkv, ATT_GW - nq), kv_index)],
        out_shape=[jax.ShapeDtypeStruct((batch, dil, seq // dil, ATT_GW), BF16),
                   jax.ShapeDtypeStruct((batch, kv_rows, ATT_GW - nq), F32)],
        scratch_shapes=[pltpu.VMEM((2 * tn // LANES, t, LANES), F32)] * 2,
        compiler_params=_cparams("parallel", "arbitrary"),
        name="attn_proj_g%d" % gi,
    )(x, g, w)
    return perm.reshape(batch, seq, ATT_GW), kv


def _alibi_slope(group, head):
    n = len(ATT_GROUPS) * ATT_HEADS
    idx = jnp.full((ATT_BLK, ATT_BLK), group * ATT_HEADS + 1, F32) + head
    return jnp.exp2(-8.0 * idx / n)


def _attn_prompt_kernel(seq, *refs):
    ng = len(ATT_GROUPS)
    qkv = [refs[3 * g:3 * g + 3] for g in range(ng)]
    out_ref, o_s, m_s, l_s = refs[3 * ng:]
    hp = pl.program_id(1)
    lane = lax.broadcasted_iota(jnp.int32, (ATT_BLK, LANES), 1)
    first_head = lane < ATT_DH
    qi = lax.broadcasted_iota(jnp.int32, (ATT_BLK, ATT_BLK), 0)
    kj = lax.broadcasted_iota(jnp.int32, (ATT_BLK, ATT_BLK), 1)
    nt = (((1,), (1,)), ((), ()))
    ones = jnp.ones((2 * ATT_BLK, LANES), BF16)

    for g, (win, dil) in enumerate(ATT_GROUPS):
        q_ref, k_ref, v_ref = qkv[g]
        n_keys = win // dil
        ls = seq // dil
        nb = ls // ATT_BLK
        d_cur = qi - kj
        d_prev = qi + ATT_BLK - kj
        bias_cur, bias_prev = [], []
        for hh in range(2):
            slope = _alibi_slope(g, (hp * 2 + hh).astype(F32)) * LOG2E
            bias_cur.append(jnp.where(d_cur >= 0, -slope * (dil * d_cur).astype(F32), NEG_INF))
            bias_prev.append(jnp.where(d_prev <= n_keys, -slope * (dil * d_prev).astype(F32), NEG_INF))
        bias_cur = jnp.concatenate(bias_cur, axis=0)
        bias_prev = jnp.concatenate(bias_prev, axis=0)
        rows = lambda qb: slice(qb * ATT_BLK, (qb + 1) * ATT_BLK)

        bias_both = jnp.concatenate([bias_prev, bias_cur], axis=1)

        def keys(qb):
            return slice((qb - 1) * ATT_BLK, (qb + 1) * ATT_BLK) if qb % nb > 0 else rows(qb)

        for c0 in range(0, seq // ATT_BLK, ATT_INTERLEAVE):
            blocks = range(c0, c0 + ATT_INTERLEAVE)
            qm = {}
            for qb in blocks:
                qf = q_ref[rows(qb), :].astype(F32)
                qm[qb] = jnp.concatenate([jnp.where(first_head, qf, 0.0), jnp.where(first_head, 0.0, qf)],
                                         axis=0).astype(BF16)
            s = {qb: lax.dot_general(qm[qb], k_ref[keys(qb), :], nt, preferred_element_type=F32)
                 + (bias_both if qb % nb > 0 else bias_cur) for qb in blocks}
            halves = (slice(0, ATT_BLK), slice(ATT_BLK, 2 * ATT_BLK))
            m = {(qb, hh): jnp.max(s[qb][half], axis=-1, keepdims=True)
                 for qb in blocks for hh, half in enumerate(halves)}
            p = {qb: jnp.concatenate([jnp.exp2(s[qb][half] - m[qb, hh]) for hh, half in enumerate(halves)],
                                     axis=0).astype(BF16) for qb in blocks}
            for qb in blocks:
                v = v_ref[keys(qb), :]
                acc = jnp.dot(p[qb], jnp.concatenate([v, ones[:v.shape[0]]], axis=1), preferred_element_type=F32)
                blk, res = qb % nb, qb // nb
                dst = pl.ds(blk * ATT_BLK * dil + res, ATT_BLK, stride=dil) if dil > 1 else rows(qb)
                o_s[g, dst, :] = jnp.where(first_head, acc[:ATT_BLK, :LANES], acc[ATT_BLK:, :LANES])
                l_s[g, dst, :] = jnp.where(first_head, acc[:ATT_BLK, LANES:], acc[ATT_BLK:, LANES:])
                m_s[g, dst, :] = jnp.where(first_head, m[qb, 0], m[qb, 1])

    m_all = m_s[0]
    for g in range(1, ng):
        m_all = jnp.maximum(m_all, m_s[g])
    num = jnp.zeros((seq, LANES), F32)
    den = jnp.zeros((seq, LANES), F32)
    for g in range(ng):
        e = jnp.exp2(m_s[g] - m_all)
        num = num + e * o_s[g]
        den = den + e * l_s[g]
    out_ref[...] = (num / den).astype(out_ref.dtype)


def attn_prompt(qkv_groups, batch, seq):
    ng = len(ATT_GROUPS)
    hpairs = ATT_HEADS * ATT_DH // LANES
    in_specs, args = [], []
    for g in range(ng):
        for part in range(3):
            in_specs.append(pl.BlockSpec((seq, LANES), lambda b, h, part=part: (b, part * hpairs + h)))
            args.append(qkv_groups[g])
    return pl.pallas_call(
        functools.partial(_attn_prompt_kernel, seq),
        grid=(batch, hpairs),
        in_specs=in_specs,
        out_specs=pl.BlockSpec((seq, LANES), lambda b, h: (b, h)),
        out_shape=jax.ShapeDtypeStruct((batch * seq, D_MODEL), BF16),
        scratch_shapes=[pltpu.VMEM((ng, seq, LANES), F32)] * 3,
        compiler_params=_cparams("parallel", "parallel"),
        name="attn_prompt",
    )(*args)


def _attn_sample_kernel(hb, qkv_ref, slope_ref, c0_ref, c1_ref, c2_ref, o_ref):
    hg = pl.program_id(0)
    b = pl.program_id(1)
    caches = (c0_ref, c1_ref, c2_ref)
    ng = len(ATT_GROUPS)

    @pl.when(b == 0)
    def _():
        o_ref[...] = jnp.zeros_like(o_ref)

    this_seq = lax.broadcasted_iota(jnp.int32, (ATT_DH, LANES), 1) == b
    pick = jnp.where(lax.broadcasted_iota(jnp.int32, (LANES, LANES), 0) == b, 1.0, 0.0).astype(BF16)
    lane_b = [[jnp.dot(qkv_ref[0, g, part], pick, preferred_element_type=F32)
               + jnp.dot(qkv_ref[1, g, part], pick, preferred_element_type=F32)
               for part in range(3)] for g in range(ng)]

    for hh in range(hb):
        rows = slice(hh * ATT_DH, (hh + 1) * ATT_DH)
        cols = [[lane_b[g][part][rows] for part in range(3)] for g in range(ng)]
        scores, new_scores = [], []
        m = None
        for g, (_, dil) in enumerate(ATT_GROUPS):
            wb = caches[g].shape[-1]
            qc = cols[g][0]
            s = jnp.sum(caches[g][0, hh] * jnp.concatenate([qc] * (wb // LANES), axis=1),
                        axis=0, keepdims=True)
            pos = lax.broadcasted_iota(jnp.int32, (1, wb), 1)
            slope = slope_ref[g, hg * hb + hh] * LOG2E
            s = jnp.where((pos & (dil - 1)) == 0, s - slope * (wb - pos).astype(F32), NEG_INF)
            s_new = jnp.sum(qc * cols[g][1], axis=0, keepdims=True)[:, :1]
            mg = jnp.maximum(jnp.max(s, axis=1, keepdims=True), s_new)
            m = mg if m is None else jnp.maximum(m, mg)
            scores.append(s)
            new_scores.append(s_new)
        l = jnp.zeros((1, 1), F32)
        o = jnp.zeros((ATT_DH, 1), F32)
        for g in range(ng):
            p = jnp.exp2(scores[g] - m)
            p_new = jnp.exp2(new_scores[g] - m)
            l = l + jnp.sum(p, axis=1, keepdims=True) + p_new
            o = o + jnp.sum(caches[g][1, hh] * p, axis=1, keepdims=True) + p_new * cols[g][2]
        o_ref[rows, :] = jnp.where(this_seq, o / l, o_ref[rows, :])


def attn_sample(proj, caches, layer, hb=16):
    n = proj.shape[0]
    ng = len(ATT_GROUPS)
    hd = ATT_HEADS * ATT_DH
    assert n == LANES
    for c, (win, dil) in zip(caches, ATT_GROUPS):
        assert c.shape[2] == win and win % dil == 0 and dil & (dil - 1) == 0
    nall = ng * ATT_HEADS
    slopes = jnp.exp2(-8.0 * jnp.arange(1, nall + 1, dtype=F32) / nall).reshape(ng, ATT_HEADS)
    proj_hi = proj.astype(BF16)
    proj_lo = (proj - proj_hi.astype(F32)).astype(BF16)
    qkv_t = jnp.stack([proj_hi.T, proj_lo.T]).reshape(2, ng, 3, hd, n)
    cache_t = [jnp.transpose(c, (0, 1, 3, 4, 5, 2)) for c in caches]
    cache_specs = [pl.BlockSpec((None, None, 2, hb, ATT_DH, c.shape[-1]), lambda h, b: (layer, b, 0, h, 0, 0))
                   for c in cache_t]
    out_t = pl.pallas_call(
        functools.partial(_attn_sample_kernel, hb),
        grid=(ATT_HEADS // hb, n),
        in_specs=[pl.BlockSpec((2, ng, 3, hb * ATT_DH, n), lambda h, b: (0, 0, 0, h, 0)),
                  pl.BlockSpec(memory_space=pltpu.SMEM)] + cache_specs,
        out_specs=pl.BlockSpec((hb * ATT_DH, n), lambda h, b: (h, 0)),
        out_shape=jax.ShapeDtypeStruct((hd, n), F32),
        compiler_params=_cparams("parallel", "arbitrary"),
        name="attn_sample",
    )(qkv_t, slopes, *cache_t)
    return out_t.T


def kernel(x_prompt, x_sample, state_ret, state_pool, cache_kv_g0, cache_kv_g1, cache_kv_g2, norm_gains,
           w_in_ret, w_out_ret, w_grp_pool, scale_pool, w_in_attn, w_out_attn, w_ffn_in, w_ffn_out):
    batch, seq, d = x_prompt.shape
    ns = x_sample.shape[0]
    caches = (cache_kv_g0, cache_kv_g1, cache_kv_g2)
    ng = len(ATT_GROUPS)
    xp = x_prompt.reshape(batch * seq, d)
    xs = x_sample.reshape(ns, d)
    ret_p, pool_p, pool_s = [], [], []
    ret_s = None
    kv_p = [[] for _ in range(ng)]
    kv_s = [[] for _ in range(ng)]

    col = jnp.arange(w_in_attn.shape[-1]) % ATT_GW
    w_in_attn = w_in_attn * jnp.where(col < ATT_HEADS * ATT_DH, ATT_Q_SCALE, 1.0).astype(F32)
    w_in_ret, w_out_ret, w_grp_pool, w_in_attn, w_out_attn, w_ffn_in, w_ffn_out = (
        w.astype(BF16) for w in (w_in_ret, w_out_ret, w_grp_pool, w_in_attn, w_out_attn, w_ffn_in, w_ffn_out))

    for i in range(DEPTH):
        kind, j = i % N_MIXERS, i // N_MIXERS
        g = [norm_gains[i, r][None, :] for r in range(4)]
        mix_p = mix_s = None
        if kind == 0:
            proj = norm_matmul(xp, g[0], w_in_ret, j, BF16, 1024)
            o, sp = ret_prompt(proj, batch, seq)
            mix_p = (o, w_out_ret, j, g[1])
            proj_s = norm_matmul(xs, g[0], w_in_ret, j, F32, ns)
            o_s, ret_s = ret_sample(proj_s, state_ret, j, ret_s)
            mix_s = (o_s, w_out_ret, j, g[1])
            ret_p.append(sp)
        elif kind == 1:
            scale = scale_pool[j][None, :]
            xp3, last = pool_prompt(xp.reshape(batch, seq, d), g[0], w_grp_pool, j, scale, g[1])
            xp = xp3.reshape(batch * seq, d)
            xs, nst = pool_sample(xs, state_pool[j].reshape(ns, POOL_PREV * d), g[0], w_grp_pool, j, scale, g[1])
            pool_p.append(last[:, POOL_HALO - POOL_PREV:])
            pool_s.append(nst.reshape(ns, POOL_PREV, d))
        else:
            x3 = xp.reshape(batch, seq, d)
            qkv = []
            for gi in range(ng):
                perm, kv = attn_proj(x3, g[0], w_in_attn, j, gi)
                qkv.append(perm.reshape(batch * seq, ATT_GW))
                kv_p[gi].append(kv.reshape(batch, kv.shape[1], 2, ATT_HEADS, ATT_DH))
            mix_p = (attn_prompt(qkv, batch, seq), w_out_attn, j, g[1])
            proj_s = norm_matmul(xs, g[0], w_in_attn, j, F32, ns)
            mix_s = (attn_sample(proj_s, caches, j).astype(BF16), w_out_attn, j, g[1])
            p5 = proj_s.reshape(ns, 1, ng, 3, ATT_HEADS, ATT_DH)
            for gi in range(ng):
                kv_s[gi].append(p5[:, :, gi, 1:])
        xp = ffn(xp, g[2], w_ffn_in, w_ffn_out, i, g[3], 1024 if mix_p is None else 512, mixer=mix_p)
        xs = ffn(xs, g[2], w_ffn_in, w_ffn_out, i, g[3], ns, mixer=mix_s)

    return (xp.reshape(batch, seq, d), xs.reshape(ns, 1, d),
            jnp.stack(ret_p), ret_s,
            jnp.stack(pool_p), jnp.stack(pool_s),
            jnp.stack(kv_p[0]), jnp.stack(kv_s[0]),
            jnp.stack(kv_p[1]), jnp.stack(kv_s[1]),
            jnp.stack(kv_p[2]), jnp.stack(kv_s[2]))
```
